```python
import jax
import jax.numpy as jnp
from jax import lax
import numpy as np


D_MODEL = 1024
BATCH = 4
SEQ = 8192
DEPTH = 4
DEC_BATCH = 32
DEC_SEQ = 16
PAST_LEN = 4096

CHUNK = 64
RET_HEADS = 4
RET_KEY_DIM = 128
RET_VAL_DIM = 128
RET_QK_WIDTH = RET_HEADS * RET_KEY_DIM
RET_V_WIDTH = RET_HEADS * RET_VAL_DIM
GMLP_CHUNK = 128
GMLP_GROUPS = 4
GMLP_WIDTH = 512
GMLP_GROUP_DIM = GMLP_WIDTH // GMLP_GROUPS
D_FF = 2816
ROPE_BASE = 10000.0
EPS = 1e-6
IN_COLS = 2 * RET_QK_WIDTH + 2 * RET_V_WIDTH + 2 * GMLP_WIDTH + 2 * D_MODEL

kernel_name = 'retention_gmlp_macaron_streaming_step'


def rms_norm(x, w):
    xf = x.astype(jnp.float32)
    y = xf * lax.rsqrt(jnp.mean(xf * xf, axis=-1, keepdims=True) + EPS)
    return (y * w.astype(jnp.float32)).astype(x.dtype)


def layer_norm(x, w, b):
    xf = x.astype(jnp.float32)
    xc = xf - jnp.mean(xf, axis=-1, keepdims=True)
    y = xc * lax.rsqrt(jnp.mean(xc * xc, axis=-1, keepdims=True) + EPS)
    return (y * w.astype(jnp.float32) + b.astype(jnp.float32)).astype(x.dtype)


def swiglu_ffn(x, w_gu, w_down):
    a, b = jnp.split(x @ w_gu, 2, axis=-1)
    return (jax.nn.silu(a) * b) @ w_down


def rotary(x, pos):
    half = x.shape[-1] // 2
    freqs = ROPE_BASE ** (-jnp.arange(half, dtype=jnp.float32) / half)
    ang = pos[:, None] * freqs[None, :]
    cos = jnp.cos(ang)[None, :, None, :]
    sin = jnp.sin(ang)[None, :, None, :]
    x1, x2 = x[..., :half], x[..., half:]
    return jnp.concatenate([x1 * cos - x2 * sin, x1 * sin + x2 * cos], axis=-1)


def retention_log_decay():
    return jnp.log1p(-jnp.exp2(-5.0 - jnp.arange(RET_HEADS, dtype=jnp.float32)))


def multiscale_retention(q, k, v, s0):
    bsz, length, n_heads, dk = q.shape
    dv = v.shape[-1]
    c = min(CHUNK, length)
    n = length // c
    log_g = retention_log_decay()
    idx = jnp.arange(c, dtype=jnp.float32)
    diff = idx[:, None] - idx[None, :]
    decay = jnp.exp(jnp.where((diff >= 0)[None], diff[None] * log_g[:, None, None], -jnp.inf))
    qc = q.reshape(bsz, n, c, n_heads, dk)
    kc = k.reshape(bsz, n, c, n_heads, dk)
    vc = v.reshape(bsz, n, c, n_heads, dv)
    scores = jnp.einsum('bnihd,bnjhd->bnhij', qc, kc) * decay
    intra = jnp.einsum('bnhij,bnjhe->bnihe', scores, vc)
    k_decay = jnp.exp((c - 1.0 - idx)[:, None] * log_g[None, :])
    kv = jnp.einsum('bnjhd,jh,bnjhe->nbhde', kc, k_decay, vc)
    block_decay = jnp.exp(c * log_g)[None, :, None, None]

    def step(s, kv_n):
        return block_decay * s + kv_n, s

    s_final, s_prev = lax.scan(step, s0, kv)
    q_decay = jnp.exp((idx + 1.0)[:, None] * log_g[None, :])
    cross = jnp.einsum('bnihd,ih,nbhde->bnihe', qc, q_decay, s_prev)
    return (intra + cross).reshape(bsz, length, n_heads, dv), s_final


def spatial_gating(u, v_n, ws, bs):
    bsz, length, _ = v_n.shape
    c = min(GMLP_CHUNK, length)
    n = length // c
    vg = v_n.reshape(bsz, n, c, GMLP_GROUPS, GMLP_GROUP_DIM)
    mask = jnp.tril(jnp.ones((c, c), dtype=bool))
    w = jnp.where(mask[None], ws[:, :c, :c], 0.0)
    s = jnp.einsum('gij,bnjgd->bnigd', w, vg) + bs[:, :c].T[None, None, :, :, None]
    return u * s.reshape(bsz, length, GMLP_WIDTH)


def token_mixing(h, pos, s0, w_in, ret_gn_w, gmlp_ln_w, gmlp_ln_b, gmlp_ws, gmlp_bs, w_br_ret, w_br_mlp):
    bsz, length, _ = h.shape
    proj = h @ w_in
    sizes = (RET_QK_WIDTH, RET_QK_WIDTH, RET_V_WIDTH, RET_V_WIDTH, GMLP_WIDTH, GMLP_WIDTH, D_MODEL)
    cuts = [int(c) for c in np.cumsum(sizes)]
    q, k, v_r, g_r, u, v_m, a_ret, a_mlp = jnp.split(proj, cuts, axis=-1)
    f32 = jnp.float32
    qh = rotary(q.reshape(bsz, length, RET_HEADS, RET_KEY_DIM).astype(f32), pos)
    kh = rotary(k.reshape(bsz, length, RET_HEADS, RET_KEY_DIM).astype(f32), pos) * (RET_KEY_DIM ** -0.5)
    vh = v_r.reshape(bsz, length, RET_HEADS, RET_VAL_DIM).astype(f32)
    o, s_new = multiscale_retention(qh, kh, vh, s0.astype(f32))
    oc = o - jnp.mean(o, axis=-1, keepdims=True)
    o = oc * lax.rsqrt(jnp.mean(oc * oc, axis=-1, keepdims=True) + EPS)
    o = o.reshape(bsz, length, RET_V_WIDTH) * ret_gn_w.astype(f32)
    y_ret = (o * jax.nn.silu(g_r.astype(f32))).astype(h.dtype) @ w_br_ret
    v_n = layer_norm(v_m, gmlp_ln_w, gmlp_ln_b)
    y_mlp = spatial_gating(u, v_n, gmlp_ws, gmlp_bs) @ w_br_mlp
    merged = jax.nn.sigmoid(a_ret) * y_ret + jax.nn.sigmoid(a_mlp) * y_mlp
    return merged, s_new.astype(h.dtype), v_n


def setup_inputs(seed: int = 0) -> dict:
    key = jax.random.key(seed)
    ks = jax.random.split(key, 24)
    f32 = jnp.float32
    nrm = lambda k, shape, scale: jax.random.normal(k, shape, f32) * scale
    gain = lambda k, shape: 1.0 + 0.02 * jax.random.normal(k, shape, f32)
    return {
        'x_prompt': nrm(ks[0], (BATCH, SEQ, D_MODEL), 1.0),
        'x_sample': nrm(ks[1], (DEC_BATCH, DEC_SEQ, D_MODEL), 1.0),
        'state_ret': nrm(ks[2], (DEPTH, DEC_BATCH, RET_HEADS, RET_KEY_DIM, RET_VAL_DIM), 0.3),
        'ffn1_norm': gain(ks[3], (DEPTH, D_MODEL)),
        'ffn1_w_gu': nrm(ks[4], (DEPTH, D_MODEL, 2 * D_FF), D_MODEL ** -0.5),
        'ffn1_w_down': nrm(ks[5], (DEPTH, D_FF, D_MODEL), D_FF ** -0.5),
        'mix_norm': gain(ks[6], (DEPTH, D_MODEL)),
        'w_in': nrm(ks[7], (DEPTH, D_MODEL, IN_COLS), D_MODEL ** -0.5),
        'ret_gn_w': gain(ks[8], (DEPTH, RET_V_WIDTH)),
        'gmlp_ln_w': gain(ks[9], (DEPTH, GMLP_WIDTH)),
        'gmlp_ln_b': nrm(ks[10], (DEPTH, GMLP_WIDTH), 0.02),
        'gmlp_ws': nrm(ks[11], (DEPTH, GMLP_GROUPS, GMLP_CHUNK, GMLP_CHUNK), GMLP_CHUNK ** -0.5),
        'gmlp_bs': gain(ks[12], (DEPTH, GMLP_GROUPS, GMLP_CHUNK)),
        'w_br_ret': nrm(ks[13], (DEPTH, RET_V_WIDTH, D_MODEL), RET_V_WIDTH ** -0.5),
        'w_br_mlp': nrm(ks[14], (DEPTH, GMLP_WIDTH, D_MODEL), GMLP_WIDTH ** -0.5),
        'w_out': nrm(ks[15], (DEPTH, D_MODEL, D_MODEL), D_MODEL ** -0.5),
        'ffn2_norm': gain(ks[16], (DEPTH, D_MODEL)),
        'ffn2_w_gu': nrm(ks[17], (DEPTH, D_MODEL, 2 * D_FF), D_MODEL ** -0.5),
        'ffn2_w_down': nrm(ks[18], (DEPTH, D_FF, D_MODEL), D_FF ** -0.5),
        'final_norm': gain(ks[19], (D_MODEL,)),
    }


def reference(x_prompt, x_sample, state_ret, ffn1_norm, ffn1_w_gu, ffn1_w_down, mix_norm, w_in, ret_gn_w, gmlp_ln_w, gmlp_ln_b, gmlp_ws, gmlp_bs, w_br_ret, w_br_mlp, w_out, ffn2_norm, ffn2_w_gu, ffn2_w_down, final_norm):
    def layer(x, pos, s0, l):
        x = x + 0.5 * swiglu_ffn(rms_norm(x, ffn1_norm[l]), ffn1_w_gu[l], ffn1_w_down[l])
        merged, s_new, v_rows = token_mixing(rms_norm(x, mix_norm[l]), pos, s0, w_in[l], ret_gn_w[l], gmlp_ln_w[l], gmlp_ln_b[l], gmlp_ws[l], gmlp_bs[l], w_br_ret[l], w_br_mlp[l])
        x = x + merged @ w_out[l]
        x = x + 0.5 * swiglu_ffn(rms_norm(x, ffn2_norm[l]), ffn2_w_gu[l], ffn2_w_down[l])
        return x, s_new, v_rows

    bp, lp = x_prompt.shape[0], x_prompt.shape[1]
    ls = x_sample.shape[1]
    pos_p = jnp.arange(lp, dtype=jnp.float32)
    pos_s = PAST_LEN + jnp.arange(ls, dtype=jnp.float32)
    s0_p = jnp.zeros((bp, RET_HEADS, RET_KEY_DIM, RET_VAL_DIM), jnp.float32)
    xp, xs = x_prompt, x_sample
    sp_list, ss_list, vs_list = [], [], []
    for l in range(DEPTH):
        xp, sp, _ = layer(xp, pos_p, s0_p, l)
        xs, ss, vs = layer(xs, pos_s, state_ret[l], l)
        sp_list.append(sp)
        ss_list.append(ss)
        vs_list.append(vs)
    y_prompt = rms_norm(xp, final_norm)
    y_sample = rms_norm(xs, final_norm)
    state_ret_prompt = jnp.stack(sp_list)
    state_ret_sample = jnp.stack(ss_list)
    gmlp_v_sample = jnp.stack(vs_list)
    return (y_prompt, y_sample, state_ret_prompt, state_ret_sample, gmlp_v_sample)
```

```python
import functools

import jax
import jax.numpy as jnp
from jax import lax
from jax.experimental import pallas as pl
from jax.experimental.pallas import tpu as pltpu

D_MODEL = 1024
DEPTH = 4
PAST_LEN = 4096
CHUNK = 64
RET_HEADS = 4
RET_KEY_DIM = 128
RET_VAL_DIM = 128
RET_WIDTH = RET_HEADS * RET_KEY_DIM
GMLP_CHUNK = 128
GMLP_GROUPS = 4
GMLP_WIDTH = 512
GMLP_GROUP_DIM = GMLP_WIDTH // GMLP_GROUPS
D_FF = 2816
ROPE_BASE = 10000.0
EPS = 1e-6
IN_COLS = 4 * RET_WIDTH + 2 * GMLP_WIDTH + 2 * D_MODEL

LANES = 128
FF_CHUNK = 256
VMEM_LIMIT_BYTES = 56 * 1024 * 1024

F32 = jnp.float32
BF16 = jnp.bfloat16


def _const_spec(shape, layer=None):
    nd = len(shape)
    if layer is None:
        return pl.BlockSpec(shape, lambda *_: (0,) * nd, pipeline_mode=pl.Buffered(1))
    return pl.BlockSpec((None,) + shape, lambda *_: (layer,) + (0,) * nd,
                        pipeline_mode=pl.Buffered(1))


def _rms(x, w):
    return x * lax.rsqrt(jnp.mean(x * x, axis=-1, keepdims=True) + EPS) * w


def _ffn_kernel(x_ref, nw_ref, wgu_ref, wd_ref, fw_ref, o_ref, h_ref, act_ref, *, final):
    h_ref[...] = _rms(x_ref[...], nw_ref[...]).astype(BF16)
    for c in range(D_FF // FF_CHUNK):
        ab = jnp.dot(h_ref[...], wgu_ref[:, 2 * c * FF_CHUNK:(2 * c + 2) * FF_CHUNK],
                     preferred_element_type=F32)
        a = ab[:, :FF_CHUNK]
        b = ab[:, FF_CHUNK:]
        act_ref[:, c * FF_CHUNK:(c + 1) * FF_CHUNK] = (a * jax.nn.sigmoid(a) * b).astype(BF16)
    y = jnp.dot(act_ref[...], wd_ref[...], preferred_element_type=F32)
    out = x_ref[...] + 0.5 * y
    if final:
        out = _rms(out, fw_ref[...])
    o_ref[...] = out


def _ffn(x, norm_w, w_gu, w_down, final_w, layer, *, tm, final):
    n = x.shape[0]
    return pl.pallas_call(
        functools.partial(_ffn_kernel, final=final),
        out_shape=jax.ShapeDtypeStruct((n, D_MODEL), F32),
        grid=(n // tm,),
        in_specs=[
            pl.BlockSpec((tm, D_MODEL), lambda i: (i, 0)),
            _const_spec((1, D_MODEL), layer),
            _const_spec((D_MODEL, 2 * D_FF), layer),
            _const_spec((D_FF, D_MODEL), layer),
            _const_spec((1, D_MODEL)),
        ],
        out_specs=pl.BlockSpec((tm, D_MODEL), lambda i: (i, 0)),
        scratch_shapes=[pltpu.VMEM((tm, D_MODEL), BF16), pltpu.VMEM((tm, D_FF), BF16)],
        compiler_params=pltpu.CompilerParams(
            dimension_semantics=("arbitrary",), vmem_limit_bytes=VMEM_LIMIT_BYTES),
        name="ffn_final" if final else "ffn",
    )(x, norm_w, w_gu, w_down, final_w)


def _table_kernel(freq_ref, logg_ref, cos_ref, sin_ref, dq_ref, dk_ref, *, pos0, period, seg):
    n = cos_ref.shape[0]
    i = pl.program_id(0) * n + lax.broadcasted_iota(jnp.int32, (n, LANES), 0)
    lane = lax.broadcasted_iota(jnp.int32, (n, LANES), 1)
    pos = (pos0 + jnp.bitwise_and(i, period - 1)).astype(F32)
    ang = pos * freq_ref[...]
    cos_ref[...] = jnp.cos(ang)
    s = jnp.sin(ang)
    sin_ref[...] = jnp.where(lane < LANES // 2, -s, s)
    m = dq_ref.shape[0]
    j = jnp.bitwise_and(lax.broadcasted_iota(jnp.int32, (m, RET_WIDTH), 0), seg - 1).astype(F32)
    e = (j + 1.0) * logg_ref[...]
    dq_ref[...] = jnp.exp(e)
    dk_ref[...] = jnp.exp(-e) * (RET_KEY_DIM ** -0.5)


def _tables(freq2, logg_lanes, *, n_pos, pos0, period, tm, seg):
    assert period & (period - 1) == 0 and seg & (seg - 1) == 0
    tb = min(n_pos, 512)
    return pl.pallas_call(
        functools.partial(_table_kernel, pos0=pos0, period=period, seg=seg),
        out_shape=(
            jax.ShapeDtypeStruct((n_pos, LANES), F32),
            jax.ShapeDtypeStruct((n_pos, LANES), F32),
            jax.ShapeDtypeStruct((tm, RET_WIDTH), F32),
            jax.ShapeDtypeStruct((tm, RET_WIDTH), F32),
        ),
        grid=(n_pos // tb,),
        in_specs=[pl.BlockSpec((1, LANES), lambda i: (0, 0)),
                  pl.BlockSpec((1, RET_WIDTH), lambda i: (0, 0))],
        out_specs=(
            pl.BlockSpec((tb, LANES), lambda i: (i, 0)),
            pl.BlockSpec((tb, LANES), lambda i: (i, 0)),
            pl.BlockSpec((tm, RET_WIDTH), lambda i: (0, 0)),
            pl.BlockSpec((tm, RET_WIDTH), lambda i: (0, 0)),
        ),
        compiler_params=pltpu.CompilerParams(dimension_semantics=("arbitrary",)),
        name="tables",
    )(freq2, logg_lanes)


def _mix_kernel(*refs, tm, dense, seg, gseg, rows, carry, has_s0, emit_vn):
    it = iter(refs)
    x_ref, cos_ref, sin_ref, dq_ref, dk_ref = (next(it) for _ in range(5))
    nw_ref, gnw_ref, lnw_ref, lnb_ref, ws_ref, bst_ref = (next(it) for _ in range(6))
    win_ref, wbr_ref, wbm_ref, wout_ref = (next(it) for _ in range(4))
    s0_ref = next(it) if has_s0 else None
    o_ref, st_ref = next(it), next(it)
    vn_ref = next(it) if emit_vn else None
    h_ref, qd_ref, kd_ref, v_ref, sg_ref, ret_ref, mrg_ref, u_ref, gat_ref = (next(it) for _ in range(9))
    if not emit_vn:
        vn_ref = next(it)

    if carry:
        @pl.when(pl.program_id(1) == 0)
        def _():
            st_ref[...] = s0_ref[...] if has_s0 else jnp.zeros(st_ref.shape, F32)
    else:
        st_ref[...] = s0_ref[...] if has_s0 else jnp.zeros(st_ref.shape, F32)

    h_ref[...] = _rms(x_ref[...], nw_ref[...]).astype(BF16)

    qk = jnp.dot(h_ref[...], win_ref[:, 0:2 * RET_WIDTH], preferred_element_type=F32)
    cos = cos_ref[...]
    sin = sin_ref[...]
    for hh in range(RET_HEADS):
        sl = slice(hh * LANES, (hh + 1) * LANES)
        q = qk[:, hh * LANES:(hh + 1) * LANES]
        k = qk[:, RET_WIDTH + hh * LANES:RET_WIDTH + (hh + 1) * LANES]
        q = q * cos + pltpu.roll(q, LANES // 2, 1) * sin
        k = k * cos + pltpu.roll(k, LANES // 2, 1) * sin
        qd_ref[:, sl] = (q * dq_ref[:, sl]).astype(BF16)
        kd_ref[:, sl] = (k * dk_ref[:, sl]).astype(BF16)

    vg = jnp.dot(h_ref[...], win_ref[:, 2 * RET_WIDTH:4 * RET_WIDTH], preferred_element_type=F32)
    v_ref[...] = vg[:, :RET_WIDTH].astype(BF16)
    g = vg[:, RET_WIDTH:]
    sg_ref[...] = g * jax.nn.sigmoid(g)

    ri = lax.broadcasted_iota(jnp.int32, (dense, dense), 0)
    ci = lax.broadcasted_iota(jnp.int32, (dense, dense), 1)
    mask = ci <= ri
    if seg < dense:
        shift = seg.bit_length() - 1
        mask = mask & ((ri >> shift) == (ci >> shift))
    n_seg = dense // seg
    for blk in range(tm // dense):
        r0 = blk * dense
        for hh in range(RET_HEADS):
            sl = slice(hh * LANES, (hh + 1) * LANES)
            qd = qd_ref[r0:r0 + dense, sl]
            kd = kd_ref[r0:r0 + dense, sl]
            vv = v_ref[r0:r0 + dense, sl]
            sc = lax.dot_general(qd, kd, (((1,), (1,)), ((), ())), preferred_element_type=F32)
            sc = jnp.where(mask, sc, 0.0).astype(BF16)
            intra = jnp.dot(sc, vv, preferred_element_type=F32)
            gam_seg = (1.0 - 2.0 ** (-5.0 - hh)) ** seg
            for sg_i in range(n_seg):
                a0 = sg_i * seg
                row = (blk * n_seg + sg_i) if rows > 1 else 0
                st = st_ref[row, hh]
                cross = jnp.dot(qd[a0:a0 + seg], st.astype(BF16), preferred_element_type=F32)
                ret_ref[r0 + a0:r0 + a0 + seg, sl] = intra[a0:a0 + seg] + cross
                kv = lax.dot_general(kd[a0:a0 + seg], vv[a0:a0 + seg], (((0,), (0,)), ((), ())),
                                     preferred_element_type=F32)
                st_ref[row, hh] = gam_seg * (st + kv)

    for hh in range(RET_HEADS):
        sl = slice(hh * LANES, (hh + 1) * LANES)
        o = ret_ref[:, sl]
        oc = o - jnp.mean(o, axis=-1, keepdims=True)
        on = oc * lax.rsqrt(jnp.mean(oc * oc, axis=-1, keepdims=True) + EPS)
        gat_ref[:, sl] = (on * gnw_ref[:, sl] * sg_ref[:, sl]).astype(BF16)
    y_ret = jnp.dot(gat_ref[...], wbr_ref[...], preferred_element_type=F32)
    a_ret = jnp.dot(h_ref[...], win_ref[:, 4 * RET_WIDTH + 2 * GMLP_WIDTH:4 * RET_WIDTH + 2 * GMLP_WIDTH + D_MODEL],
                    preferred_element_type=F32)
    mrg_ref[...] = jax.nn.sigmoid(a_ret) * y_ret

    uv = jnp.dot(h_ref[...], win_ref[:, 4 * RET_WIDTH:4 * RET_WIDTH + 2 * GMLP_WIDTH],
                 preferred_element_type=F32)
    u_ref[...] = uv[:, :GMLP_WIDTH]
    vm = uv[:, GMLP_WIDTH:]
    vc = vm - jnp.mean(vm, axis=-1, keepdims=True)
    vn = vc * lax.rsqrt(jnp.mean(vc * vc, axis=-1, keepdims=True) + EPS)
    vn_ref[...] = vn * lnw_ref[...] + lnb_ref[...]

    n_g = tm // gseg
    wi = lax.broadcasted_iota(jnp.int32, (gseg, gseg), 0)
    wj = lax.broadcasted_iota(jnp.int32, (gseg, gseg), 1)
    for gg in range(GMLP_GROUPS):
        sl = slice(gg * LANES, (gg + 1) * LANES)
        w = jnp.where(wj <= wi, ws_ref[gg, 0:gseg, 0:gseg], 0.0).astype(BF16)
        rhs = jnp.concatenate([vn_ref[c * gseg:(c + 1) * gseg, sl] for c in range(n_g)], axis=1)
        sp = jnp.dot(w, rhs.astype(BF16), preferred_element_type=F32)
        bias = bst_ref[0:gseg, gg:gg + 1]
        for c in range(n_g):
            s_c = sp[:, c * LANES:(c + 1) * LANES] + bias
            gat_ref[c * gseg:(c + 1) * gseg, sl] = (u_ref[c * gseg:(c + 1) * gseg, sl] * s_c).astype(BF16)
    y_mlp = jnp.dot(gat_ref[...], wbm_ref[...], preferred_element_type=F32)
    a_mlp = jnp.dot(h_ref[...], win_ref[:, 4 * RET_WIDTH + 2 * GMLP_WIDTH + D_MODEL:IN_COLS],
                    preferred_element_type=F32)
    merged = (mrg_ref[...] + jax.nn.sigmoid(a_mlp) * y_mlp).astype(BF16)
    o_ref[...] = x_ref[...] + jnp.dot(merged, wout_ref[...], preferred_element_type=F32)


def _mix(x, tabs, norm_w, gn_w, ln_w, ln_b, ws, bst, w_in, w_br, w_bm, w_out, s0, layer, *,
         tm, dense, seg, gseg, rows, n_outer, n_inner, carry, emit_vn):
    cos, sin, dq, dk = tabs
    has_s0 = s0 is not None
    n_tok = x.shape[0]
    n_state = n_outer * rows
    x_spec = pl.BlockSpec((tm, D_MODEL), lambda b, t: (b * n_inner + t, 0))
    if cos.shape[0] == tm:
        tab_spec = pl.BlockSpec((tm, LANES), lambda b, t: (0, 0))
    else:
        tab_spec = pl.BlockSpec((tm, LANES), lambda b, t: (t, 0))
    st_spec = pl.BlockSpec((rows, RET_HEADS, RET_KEY_DIM, RET_VAL_DIM), lambda b, t: (b, 0, 0, 0))
    in_specs = [
        x_spec, tab_spec, tab_spec,
        _const_spec((tm, RET_WIDTH)), _const_spec((tm, RET_WIDTH)),
        _const_spec((1, D_MODEL), layer), _const_spec((1, RET_WIDTH), layer),
        _const_spec((1, GMLP_WIDTH), layer), _const_spec((1, GMLP_WIDTH), layer),
        _const_spec((GMLP_GROUPS, GMLP_CHUNK, GMLP_CHUNK), layer),
        _const_spec((GMLP_CHUNK, GMLP_GROUPS), layer),
        _const_spec((D_MODEL, IN_COLS), layer), _const_spec((RET_WIDTH, D_MODEL), layer),
        _const_spec((GMLP_WIDTH, D_MODEL), layer), _const_spec((D_MODEL, D_MODEL), layer),
    ]
    args = [x, cos, sin, dq, dk, norm_w, gn_w, ln_w, ln_b, ws, bst, w_in, w_br, w_bm, w_out]
    if has_s0:
        in_specs.append(pl.BlockSpec((None, rows, RET_HEADS, RET_KEY_DIM, RET_VAL_DIM),
                                     lambda b, t: (layer, b, 0, 0, 0)))
        args.append(s0)
    out_shape = [jax.ShapeDtypeStruct((n_tok, D_MODEL), F32),
                 jax.ShapeDtypeStruct((n_state, RET_HEADS, RET_KEY_DIM, RET_VAL_DIM), F32)]
    out_specs = [x_spec, st_spec]
    scratch = [
        pltpu.VMEM((tm, D_MODEL), BF16),
        pltpu.VMEM((tm, RET_WIDTH), BF16),
        pltpu.VMEM((tm, RET_WIDTH), BF16),
        pltpu.VMEM((tm, RET_WIDTH), BF16),
        pltpu.VMEM((tm, RET_WIDTH), F32),
        pltpu.VMEM((tm, RET_WIDTH), F32),
        pltpu.VMEM((tm, D_MODEL), F32),
        pltpu.VMEM((tm, GMLP_WIDTH), F32),
        pltpu.VMEM((tm, GMLP_WIDTH), BF16),
    ]
    if emit_vn:
        out_shape.append(jax.ShapeDtypeStruct((n_tok, GMLP_WIDTH), F32))
        out_specs.append(pl.BlockSpec((tm, GMLP_WIDTH), lambda b, t: (b * n_inner + t, 0)))
    else:
        scratch.append(pltpu.VMEM((tm, GMLP_WIDTH), F32))
    return pl.pallas_call(
        functools.partial(_mix_kernel, tm=tm, dense=dense, seg=seg, gseg=gseg, rows=rows,
                          carry=carry, has_s0=has_s0, emit_vn=emit_vn),
        out_shape=out_shape,
        grid=(n_outer, n_inner),
        in_specs=in_specs,
        out_specs=out_specs,
        scratch_shapes=scratch,
        compiler_params=pltpu.CompilerParams(
            dimension_semantics=("arbitrary", "arbitrary"), vmem_limit_bytes=VMEM_LIMIT_BYTES),
        name="mix_rows" if rows > 1 else "mix_seq",
    )(*args)


def _interleave_gate_up(w_gu):
    d = w_gu.shape[0]
    w = w_gu.reshape(d, D_MODEL, 2, D_FF // FF_CHUNK, FF_CHUNK)
    return jnp.swapaxes(w, 2, 3).reshape(d, D_MODEL, 2 * D_FF)


def kernel(x_prompt, x_sample, state_ret, ffn1_norm, ffn1_w_gu, ffn1_w_down, mix_norm, w_in, ret_gn_w, gmlp_ln_w, gmlp_ln_b, gmlp_ws, gmlp_bs, w_br_ret, w_br_mlp, w_out, ffn2_norm, ffn2_w_gu, ffn2_w_down, final_norm):
    bp, lp, _ = x_prompt.shape
    bs, ls, _ = x_sample.shape

    wgu1 = _interleave_gate_up(ffn1_w_gu).astype(BF16)
    wgu2 = _interleave_gate_up(ffn2_w_gu).astype(BF16)
    wd1 = ffn1_w_down.astype(BF16)
    wd2 = ffn2_w_down.astype(BF16)
    w_in_b = w_in.astype(BF16)
    w_br_b = w_br_ret.astype(BF16)
    w_bm_b = w_br_mlp.astype(BF16)
    w_out_b = w_out.astype(BF16)
    n1 = ffn1_norm.reshape(DEPTH, 1, D_MODEL)
    n2 = ffn2_norm.reshape(DEPTH, 1, D_MODEL)
    nm = mix_norm.reshape(DEPTH, 1, D_MODEL)
    gnw = ret_gn_w.reshape(DEPTH, 1, RET_WIDTH)
    lnw = gmlp_ln_w.reshape(DEPTH, 1, GMLP_WIDTH)
    lnb = gmlp_ln_b.reshape(DEPTH, 1, GMLP_WIDTH)
    bst = jnp.swapaxes(gmlp_bs, 1, 2)
    fw = final_norm.reshape(1, D_MODEL)

    half = RET_KEY_DIM // 2
    freqs = ROPE_BASE ** (-jnp.arange(half, dtype=F32) / half)
    freq2 = jnp.concatenate([freqs, freqs]).reshape(1, LANES)
    log_g = jnp.log1p(-jnp.exp2(-5.0 - jnp.arange(RET_HEADS, dtype=F32)))
    logg_lanes = jnp.repeat(log_g, LANES).reshape(1, RET_WIDTH)

    tm_p = 512
    dense_p = 256
    rows_s = 8
    tm_s = rows_s * ls
    tabs_p = _tables(freq2, logg_lanes, n_pos=lp, pos0=0, period=lp, tm=tm_p, seg=dense_p)
    tabs_s = _tables(freq2, logg_lanes, n_pos=tm_s, pos0=PAST_LEN, period=ls, tm=tm_s, seg=ls)

    xp = x_prompt.reshape(bp * lp, D_MODEL)
    xs = x_sample.reshape(bs * ls, D_MODEL)
    sp_list, ss_list, vs_list = [], [], []
    for l in range(DEPTH):
        last = l == DEPTH - 1
        xp = _ffn(xp, n1, wgu1, wd1, fw, l, tm=512, final=False)
        xs = _ffn(xs, n1, wgu1, wd1, fw, l, tm=256, final=False)
        xp, sp = _mix(xp, tabs_p, nm, gnw, lnw, lnb, gmlp_ws, bst, w_in_b, w_br_b, w_bm_b, w_out_b,
                      None, l, tm=tm_p, dense=dense_p, seg=dense_p, gseg=min(GMLP_CHUNK, lp), rows=1,
                      n_outer=bp, n_inner=lp // tm_p, carry=True, emit_vn=False)
        xs, ss, vs = _mix(xs, tabs_s, nm, gnw, lnw, lnb, gmlp_ws, bst, w_in_b, w_br_b, w_bm_b, w_out_b,
                          state_ret, l, tm=tm_s, dense=tm_s, seg=ls, gseg=min(GMLP_CHUNK, ls), rows=rows_s,
                          n_outer=bs // rows_s, n_inner=1, carry=False, emit_vn=True)
        xp = _ffn(xp, n2, wgu2, wd2, fw, l, tm=512, final=last)
        xs = _ffn(xs, n2, wgu2, wd2, fw, l, tm=256, final=last)
        sp_list.append(sp)
        ss_list.append(ss)
        vs_list.append(vs.reshape(bs, ls, GMLP_WIDTH))
    y_prompt = xp.reshape(bp, lp, D_MODEL)
    y_sample = xs.reshape(bs, ls, D_MODEL)
    return (y_prompt, y_sample, jnp.stack(sp_list), jnp.stack(ss_list), jnp.stack(vs_list))
```

```python
import functools

import jax
import jax.numpy as jnp
from jax import lax
from jax.experimental import pallas as pl
from jax.experimental.pallas import tpu as pltpu

D_MODEL = 1024
DEPTH = 4
PAST_LEN = 4096
CHUNK = 64
RET_HEADS = 4
RET_KEY_DIM = 128
RET_VAL_DIM = 128
RET_WIDTH = RET_HEADS * RET_KEY_DIM
GMLP_CHUNK = 128
GMLP_GROUPS = 4
GMLP_WIDTH = 512
GMLP_GROUP_DIM = GMLP_WIDTH // GMLP_GROUPS
D_FF = 2816
ROPE_BASE = 10000.0
EPS = 1e-6
IN_COLS = 4 * RET_WIDTH + 2 * GMLP_WIDTH + 2 * D_MODEL

LANES = 128
FF_CHUNK = 256
VMEM_LIMIT_BYTES = 56 * 1024 * 1024

F32 = jnp.float32
BF16 = jnp.bfloat16


def _const_spec(shape, layer=None):
    nd = len(shape)
    if layer is None:
        return pl.BlockSpec(shape, lambda *_: (0,) * nd, pipeline_mode=pl.Buffered(1))
    return pl.BlockSpec((None,) + shape, lambda *_: (layer,) + (0,) * nd,
                        pipeline_mode=pl.Buffered(1))


def _rms_scale(x):
    return lax.rsqrt(jnp.mean(x * x, axis=-1, keepdims=True) + EPS)


def _rms(x, w):
    return x * _rms_scale(x) * w


def _ffn_kernel(*refs, n_stage, final):
    x_ref = refs[0]
    stage_refs = refs[1:1 + 2 * n_stage]
    fw_ref, o_ref, h_ref, act_ref = refs[1 + 2 * n_stage:]
    src_ref = x_ref
    for s in range(n_stage):
        wgu_ref, wd_ref = stage_refs[2 * s:2 * s + 2]
        x = src_ref[...]
        h_ref[...] = x.astype(BF16)
        rinv = _rms_scale(x)
        for c in range(D_FF // FF_CHUNK):
            cols = slice(c * FF_CHUNK, (c + 1) * FF_CHUNK)
            a = rinv * jnp.dot(h_ref[...], wgu_ref[:, cols], preferred_element_type=F32)
            b = rinv * jnp.dot(h_ref[...], wgu_ref[:, D_FF + c * FF_CHUNK:D_FF + (c + 1) * FF_CHUNK],
                               preferred_element_type=F32)
            act_ref[:, cols] = (a * jax.nn.sigmoid(a) * b).astype(BF16)
        y = jnp.dot(act_ref[...], wd_ref[...], preferred_element_type=F32)
        out = src_ref[...] + 0.5 * y
        if final and s == n_stage - 1:
            out = _rms(out, fw_ref[...])
        o_ref[...] = out
        src_ref = o_ref


def _ffn(x, stages, final_w, *, tm, final):
    n = x.shape[0]
    in_specs = [pl.BlockSpec((tm, D_MODEL), lambda i: (i, 0))]
    args = [x]
    for w_gu, w_down, layer in stages:
        in_specs += [_const_spec((D_MODEL, 2 * D_FF), layer),
                     _const_spec((D_FF, D_MODEL), layer)]
        args += [w_gu, w_down]
    in_specs.append(_const_spec((1, D_MODEL)))
    args.append(final_w)
    return pl.pallas_call(
        functools.partial(_ffn_kernel, n_stage=len(stages), final=final),
        out_shape=jax.ShapeDtypeStruct((n, D_MODEL), F32),
        grid=(n // tm,),
        in_specs=in_specs,
        out_specs=pl.BlockSpec((tm, D_MODEL), lambda i: (i, 0)),
        scratch_shapes=[pltpu.VMEM((tm, D_MODEL), BF16), pltpu.VMEM((tm, D_FF), BF16)],
        compiler_params=pltpu.CompilerParams(
            dimension_semantics=("arbitrary",), vmem_limit_bytes=VMEM_LIMIT_BYTES),
        name="ffn%d%s" % (len(stages), "_final" if final else ""),
    )(*args)


def _table_kernel(freq_ref, logg_ref, cos_ref, sin_ref, dq_ref, dk_ref, *, pos0, period, seg):
    n = cos_ref.shape[0]
    i = pl.program_id(0) * n + lax.broadcasted_iota(jnp.int32, (n, LANES), 0)
    lane = lax.broadcasted_iota(jnp.int32, (n, LANES), 1)
    pos = (pos0 + jnp.bitwise_and(i, period - 1)).astype(F32)
    ang = pos * freq_ref[...]
    cos_ref[...] = jnp.cos(ang)
    s = jnp.sin(ang)
    sin_ref[...] = jnp.where(lane < LANES // 2, -s, s)
    m = dq_ref.shape[0]
    j = jnp.bitwise_and(lax.broadcasted_iota(jnp.int32, (m, RET_WIDTH), 0), seg - 1).astype(F32)
    e = (j + 1.0) * logg_ref[...]
    dq_ref[...] = jnp.exp(e)
    dk_ref[...] = jnp.exp(-e) * (RET_KEY_DIM ** -0.5)


def _tables(freq2, logg_lanes, *, n_pos, pos0, period, tm, seg):
    assert period & (period - 1) == 0 and seg & (seg - 1) == 0
    tb = min(n_pos, 512)
    return pl.pallas_call(
        functools.partial(_table_kernel, pos0=pos0, period=period, seg=seg),
        out_shape=(
            jax.ShapeDtypeStruct((n_pos, LANES), F32),
            jax.ShapeDtypeStruct((n_pos, LANES), F32),
            jax.ShapeDtypeStruct((tm, RET_WIDTH), F32),
            jax.ShapeDtypeStruct((tm, RET_WIDTH), F32),
        ),
        grid=(n_pos // tb,),
        in_specs=[pl.BlockSpec((1, LANES), lambda i: (0, 0)),
                  pl.BlockSpec((1, RET_WIDTH), lambda i: (0, 0))],
        out_specs=(
            pl.BlockSpec((tb, LANES), lambda i: (i, 0)),
            pl.BlockSpec((tb, LANES), lambda i: (i, 0)),
            pl.BlockSpec((tm, RET_WIDTH), lambda i: (0, 0)),
            pl.BlockSpec((tm, RET_WIDTH), lambda i: (0, 0)),
        ),
        compiler_params=pltpu.CompilerParams(dimension_semantics=("arbitrary",)),
        name="tables",
    )(freq2, logg_lanes)


def _mix_kernel(*refs, tm, dense, seg, gseg, rows, carry, has_s0, emit_vn):
    it = iter(refs)
    x_ref, cos_ref, sin_ref, dq_ref, dk_ref = (next(it) for _ in range(5))
    gnw_ref, lnw_ref, lnb_ref, ws_ref, bst_ref = (next(it) for _ in range(5))
    win_ref, wbr_ref, wbm_ref, wout_ref = (next(it) for _ in range(4))
    s0_ref = next(it) if has_s0 else None
    o_ref, st_ref = next(it), next(it)
    vn_ref = next(it) if emit_vn else None
    h_ref, qd_ref, kd_ref, v_ref, sg_ref, ret_ref, mrg_ref, u_ref, gat_ref, gm_ref = (next(it) for _ in range(10))
    if not emit_vn:
        vn_ref = next(it)

    if carry:
        @pl.when(pl.program_id(1) == 0)
        def _():
            st_ref[...] = s0_ref[...] if has_s0 else jnp.zeros(st_ref.shape, F32)
    else:
        st_ref[...] = s0_ref[...] if has_s0 else jnp.zeros(st_ref.shape, F32)

    x = x_ref[...]
    h_ref[...] = x.astype(BF16)
    rinv = _rms_scale(x)

    c_uv = 4 * RET_WIDTH
    c_ar = c_uv + 2 * GMLP_WIDTH
    c_am = c_ar + D_MODEL

    def proj(lo, hi):
        return rinv * jnp.dot(h_ref[...], win_ref[:, lo:hi], preferred_element_type=F32)

    qk = proj(0, 2 * RET_WIDTH)
    cos = cos_ref[...]
    sin = sin_ref[...]
    for hh in range(RET_HEADS):
        sl = slice(hh * LANES, (hh + 1) * LANES)
        q = qk[:, hh * LANES:(hh + 1) * LANES]
        k = qk[:, RET_WIDTH + hh * LANES:RET_WIDTH + (hh + 1) * LANES]
        q = q * cos + pltpu.roll(q, LANES // 2, 1) * sin
        k = k * cos + pltpu.roll(k, LANES // 2, 1) * sin
        qd_ref[:, sl] = (q * dq_ref[:, sl]).astype(BF16)
        kd_ref[:, sl] = (k * dk_ref[:, sl]).astype(BF16)

    vg = proj(2 * RET_WIDTH, 4 * RET_WIDTH)
    v_ref[...] = vg[:, :RET_WIDTH].astype(BF16)
    g = vg[:, RET_WIDTH:]
    sg_ref[...] = g * jax.nn.sigmoid(g)

    ri = lax.broadcasted_iota(jnp.int32, (dense, dense), 0)
    ci = lax.broadcasted_iota(jnp.int32, (dense, dense), 1)
    mask = ci <= ri
    if seg < dense:
        shift = seg.bit_length() - 1
        mask = mask & ((ri >> shift) == (ci >> shift))
    n_seg = dense // seg
    n_blk = tm // dense

    def retention_block(blk):
        r0 = blk * dense
        for hh in range(RET_HEADS):
            sl = slice(hh * LANES, (hh + 1) * LANES)
            qd = qd_ref[r0:r0 + dense, sl]
            kd = kd_ref[r0:r0 + dense, sl]
            vv = v_ref[r0:r0 + dense, sl]
            sc = lax.dot_general(qd, kd, (((1,), (1,)), ((), ())), preferred_element_type=F32)
            sc = jnp.where(mask, sc, 0.0).astype(BF16)
            intra = jnp.dot(sc, vv, preferred_element_type=F32)
            gam_seg = (1.0 - 2.0 ** (-5.0 - hh)) ** seg
            for sg_i in range(n_seg):
                a0 = sg_i * seg
                row = (blk * n_seg + sg_i) if rows > 1 else 0
                st = st_ref[row, hh]
                cross = jnp.dot(qd[a0:a0 + seg], st.astype(BF16), preferred_element_type=F32)
                ret_ref[r0 + a0:r0 + a0 + seg, sl] = intra[a0:a0 + seg] + cross
                kv = lax.dot_general(kd[a0:a0 + seg], vv[a0:a0 + seg], (((0,), (0,)), ((), ())),
                                     preferred_element_type=F32)
                st_ref[row, hh] = gam_seg * (st + kv)

    for blk in range((n_blk + 1) // 2):
        retention_block(blk)

    uv = proj(c_uv, c_ar)
    u_ref[...] = uv[:, :GMLP_WIDTH]
    vm = uv[:, GMLP_WIDTH:]
    vc = vm - jnp.mean(vm, axis=-1, keepdims=True)
    vn = vc * lax.rsqrt(jnp.mean(vc * vc, axis=-1, keepdims=True) + EPS)
    vn_ref[...] = vn * lnw_ref[...] + lnb_ref[...]

    for blk in range((n_blk + 1) // 2, n_blk):
        retention_block(blk)

    mrg_ref[...] = jax.nn.sigmoid(proj(c_ar, c_am))

    for hh in range(RET_HEADS):
        sl = slice(hh * LANES, (hh + 1) * LANES)
        o = ret_ref[:, sl]
        oc = o - jnp.mean(o, axis=-1, keepdims=True)
        on = oc * lax.rsqrt(jnp.mean(oc * oc, axis=-1, keepdims=True) + EPS)
        gat_ref[:, sl] = (on * gnw_ref[:, sl] * sg_ref[:, sl]).astype(BF16)

    n_g = tm // gseg
    wi = lax.broadcasted_iota(jnp.int32, (gseg, gseg), 0)
    wj = lax.broadcasted_iota(jnp.int32, (gseg, gseg), 1)
    for gg in range(GMLP_GROUPS):
        sl = slice(gg * LANES, (gg + 1) * LANES)
        w = jnp.where(wj <= wi, ws_ref[gg, 0:gseg, 0:gseg], 0.0).astype(BF16)
        rhs = jnp.concatenate([vn_ref[c * gseg:(c + 1) * gseg, sl] for c in range(n_g)], axis=1)
        sp = jnp.dot(w, rhs.astype(BF16), preferred_element_type=F32)
        bias = bst_ref[0:gseg, gg:gg + 1]
        for c in range(n_g):
            s_c = sp[:, c * LANES:(c + 1) * LANES] + bias
            gm_ref[c * gseg:(c + 1) * gseg, sl] = (u_ref[c * gseg:(c + 1) * gseg, sl] * s_c).astype(BF16)

    g_mlp = jax.nn.sigmoid(proj(c_am, IN_COLS))
    y_ret = jnp.dot(gat_ref[...], wbr_ref[...], preferred_element_type=F32)
    y_mlp = jnp.dot(gm_ref[...], wbm_ref[...], preferred_element_type=F32)
    merged = (mrg_ref[...] * y_ret + g_mlp * y_mlp).astype(BF16)
    o_ref[...] = x_ref[...] + jnp.dot(merged, wout_ref[...], preferred_element_type=F32)


def _mix(x, tabs, gn_w, ln_w, ln_b, ws, bst, w_in, w_br, w_bm, w_out, s0, layer, *,
         tm, dense, seg, gseg, rows, n_outer, n_inner, carry, emit_vn):
    cos, sin, dq, dk = tabs
    has_s0 = s0 is not None
    n_tok = x.shape[0]
    n_state = n_outer * rows
    x_spec = pl.BlockSpec((tm, D_MODEL), lambda b, t: (b * n_inner + t, 0))
    if cos.shape[0] == tm:
        tab_spec = pl.BlockSpec((tm, LANES), lambda b, t: (0, 0))
    else:
        tab_spec = pl.BlockSpec((tm, LANES), lambda b, t: (t, 0))
    st_spec = pl.BlockSpec((rows, RET_HEADS, RET_KEY_DIM, RET_VAL_DIM), lambda b, t: (b, 0, 0, 0))
    in_specs = [
        x_spec, tab_spec, tab_spec,
        _const_spec((tm, RET_WIDTH)), _const_spec((tm, RET_WIDTH)),
        _const_spec((1, RET_WIDTH), layer),
        _const_spec((1, GMLP_WIDTH), layer), _const_spec((1, GMLP_WIDTH), layer),
        _const_spec((GMLP_GROUPS, GMLP_CHUNK, GMLP_CHUNK), layer),
        _const_spec((GMLP_CHUNK, GMLP_GROUPS), layer),
        _const_spec((D_MODEL, IN_COLS), layer), _const_spec((RET_WIDTH, D_MODEL), layer),
        _const_spec((GMLP_WIDTH, D_MODEL), layer), _const_spec((D_MODEL, D_MODEL), layer),
    ]
    args = [x, cos, sin, dq, dk, gn_w, ln_w, ln_b, ws, bst, w_in, w_br, w_bm, w_out]
    if has_s0:
        in_specs.append(pl.BlockSpec((None, rows, RET_HEADS, RET_KEY_DIM, RET_VAL_DIM),
                                     lambda b, t: (layer, b, 0, 0, 0)))
        args.append(s0)
    out_shape = [jax.ShapeDtypeStruct((n_tok, D_MODEL), F32),
                 jax.ShapeDtypeStruct((n_state, RET_HEADS, RET_KEY_DIM, RET_VAL_DIM), F32)]
    out_specs = [x_spec, st_spec]
    scratch = [
        pltpu.VMEM((tm, D_MODEL), BF16),
        pltpu.VMEM((tm, RET_WIDTH), BF16),
        pltpu.VMEM((tm, RET_WIDTH), BF16),
        pltpu.VMEM((tm, RET_WIDTH), BF16),
        pltpu.VMEM((tm, RET_WIDTH), F32),
        pltpu.VMEM((tm, RET_WIDTH), F32),
        pltpu.VMEM((tm, D_MODEL), F32),
        pltpu.VMEM((tm, GMLP_WIDTH), F32),
        pltpu.VMEM((tm, RET_WIDTH), BF16),
        pltpu.VMEM((tm, GMLP_WIDTH), BF16),
    ]
    if emit_vn:
        out_shape.append(jax.ShapeDtypeStruct((n_tok, GMLP_WIDTH), F32))
        out_specs.append(pl.BlockSpec((tm, GMLP_WIDTH), lambda b, t: (b * n_inner + t, 0)))
    else:
        scratch.append(pltpu.VMEM((tm, GMLP_WIDTH), F32))
    return pl.pallas_call(
        functools.partial(_mix_kernel, tm=tm, dense=dense, seg=seg, gseg=gseg, rows=rows,
                          carry=carry, has_s0=has_s0, emit_vn=emit_vn),
        out_shape=out_shape,
        grid=(n_outer, n_inner),
        in_specs=in_specs,
        out_specs=out_specs,
        scratch_shapes=scratch,
        compiler_params=pltpu.CompilerParams(
            dimension_semantics=("arbitrary", "arbitrary"), vmem_limit_bytes=VMEM_LIMIT_BYTES),
        name="mix_rows" if rows > 1 else "mix_seq",
    )(*args)


def kernel(x_prompt, x_sample, state_ret, ffn1_norm, ffn1_w_gu, ffn1_w_down, mix_norm, w_in, ret_gn_w, gmlp_ln_w, gmlp_ln_b, gmlp_ws, gmlp_bs, w_br_ret, w_br_mlp, w_out, ffn2_norm, ffn2_w_gu, ffn2_w_down, final_norm):
    bp, lp, _ = x_prompt.shape
    bs, ls, _ = x_sample.shape

    wgu1 = (ffn1_norm[:, :, None] * ffn1_w_gu).astype(BF16)
    wgu2 = (ffn2_norm[:, :, None] * ffn2_w_gu).astype(BF16)
    wd1 = ffn1_w_down.astype(BF16)
    wd2 = ffn2_w_down.astype(BF16)
    w_in_b = (mix_norm[:, :, None] * w_in).astype(BF16)
    w_br_b = w_br_ret.astype(BF16)
    w_bm_b = w_br_mlp.astype(BF16)
    w_out_b = w_out.astype(BF16)
    gnw = ret_gn_w.reshape(DEPTH, 1, RET_WIDTH)
    lnw = gmlp_ln_w.reshape(DEPTH, 1, GMLP_WIDTH)
    lnb = gmlp_ln_b.reshape(DEPTH, 1, GMLP_WIDTH)
    bst = jnp.swapaxes(gmlp_bs, 1, 2)
    fw = final_norm.reshape(1, D_MODEL)

    half = RET_KEY_DIM // 2
    freqs = ROPE_BASE ** (-jnp.arange(half, dtype=F32) / half)
    freq2 = jnp.concatenate([freqs, freqs]).reshape(1, LANES)
    log_g = jnp.log1p(-jnp.exp2(-5.0 - jnp.arange(RET_HEADS, dtype=F32)))
    logg_lanes = jnp.repeat(log_g, LANES).reshape(1, RET_WIDTH)

    tm_p = 512
    dense_p = 256
    rows_s = 8
    tm_s = rows_s * ls
    tabs_p = _tables(freq2, logg_lanes, n_pos=lp, pos0=0, period=lp, tm=tm_p, seg=dense_p)
    tabs_s = _tables(freq2, logg_lanes, n_pos=tm_s, pos0=PAST_LEN, period=ls, tm=tm_s, seg=ls)

    xp = x_prompt.reshape(bp * lp, D_MODEL)
    xs = x_sample.reshape(bs * ls, D_MODEL)
    sp_list, ss_list, vs_list = [], [], []
    for l in range(DEPTH):
        pre = [(wgu2, wd2, l - 1)] if l > 0 else []
        stages = pre + [(wgu1, wd1, l)]
        xp = _ffn(xp, stages, fw, tm=512, final=False)
        xs = _ffn(xs, stages, fw, tm=256, final=False)
        xp, sp = _mix(xp, tabs_p, gnw, lnw, lnb, gmlp_ws, bst, w_in_b, w_br_b, w_bm_b, w_out_b,
                      None, l, tm=tm_p, dense=dense_p, seg=dense_p, gseg=min(GMLP_CHUNK, lp), rows=1,
                      n_outer=bp, n_inner=lp // tm_p, carry=True, emit_vn=False)
        xs, ss, vs = _mix(xs, tabs_s, gnw, lnw, lnb, gmlp_ws, bst, w_in_b, w_br_b, w_bm_b, w_out_b,
                          state_ret, l, tm=tm_s, dense=tm_s, seg=ls, gseg=min(GMLP_CHUNK, ls), rows=rows_s,
                          n_outer=bs // rows_s, n_inner=1, carry=False, emit_vn=True)
        sp_list.append(sp)
        ss_list.append(ss)
        vs_list.append(vs.reshape(bs, ls, GMLP_WIDTH))
    last = [(wgu2, wd2, DEPTH - 1)]
    xp = _ffn(xp, last, fw, tm=512, final=True)
    xs = _ffn(xs, last, fw, tm=256, final=True)
    y_prompt = xp.reshape(bp, lp, D_MODEL)
    y_sample = xs.reshape(bs, ls, D_MODEL)
    return (y_prompt, y_sample, jnp.stack(sp_list), jnp.stack(ss_list), jnp.stack(vs_list))
```

```python
import functools

import jax
import jax.numpy as jnp
from jax import lax
from jax.experimental import pallas as pl
from jax.experimental.pallas import tpu as pltpu

D_MODEL = 1024
DEPTH = 4
PAST_LEN = 4096
CHUNK = 64
RET_HEADS = 4
RET_KEY_DIM = 128
RET_VAL_DIM = 128
RET_WIDTH = RET_HEADS * RET_KEY_DIM
GMLP_CHUNK = 128
GMLP_GROUPS = 4
GMLP_WIDTH = 512
GMLP_GROUP_DIM = GMLP_WIDTH // GMLP_GROUPS
D_FF = 2816
ROPE_BASE = 10000.0
EPS = 1e-6
IN_COLS = 4 * RET_WIDTH + 2 * GMLP_WIDTH + 2 * D_MODEL

LANES = 128
FF_CHUNK = 256
VMEM_LIMIT_BYTES = 56 * 1024 * 1024

F32 = jnp.float32
BF16 = jnp.bfloat16


def _const_spec(shape, layer=None):
    nd = len(shape)
    if layer is None:
        return pl.BlockSpec(shape, lambda *_: (0,) * nd, pipeline_mode=pl.Buffered(1))
    return pl.BlockSpec((None,) + shape, lambda *_: (layer,) + (0,) * nd,
                        pipeline_mode=pl.Buffered(1))


BF16_SUBLANES = 16


def _cast_specs(casts, n_steps, linear_step):
    in_specs, args, out_specs, out_shapes, has_gain = [], [], [], [], []
    for src, layer, gain in casts:
        _, n_rows, n_cols = src.shape
        rb = BF16_SUBLANES
        while n_rows % rb or n_rows // rb > n_steps:
            rb += BF16_SUBLANES
        n_blk = n_rows // rb
        hold = n_steps // n_blk

        def blk(*g, hold=hold, n_blk=n_blk):
            return jnp.minimum(linear_step(*g) // hold, n_blk - 1)

        in_specs.append(pl.BlockSpec((None, rb, n_cols), lambda *g, blk=blk, layer=layer: (layer, blk(*g), 0)))
        args.append(src)
        if gain is not None:
            in_specs.append(pl.BlockSpec((None, rb, 1), lambda *g, blk=blk, layer=layer: (layer, blk(*g), 0)))
            args.append(gain)
        out_specs.append(pl.BlockSpec((rb, n_cols), lambda *g, blk=blk: (blk(*g), 0)))
        out_shapes.append(jax.ShapeDtypeStruct((n_rows, n_cols), BF16))
        has_gain.append(gain is not None)
    return in_specs, args, out_specs, out_shapes, tuple(has_gain)


def _run_casts(in_refs, out_refs, has_gain):
    it = iter(in_refs)
    for dst_ref, gained in zip(out_refs, has_gain):
        w = next(it)[...]
        if gained:
            w = w * next(it)[...]
        dst_ref[...] = w.astype(BF16)


def _rms_scale(x):
    return lax.rsqrt(jnp.mean(x * x, axis=-1, keepdims=True) + EPS)


def _rms(x, w):
    return x * _rms_scale(x) * w


def _ffn_kernel(*refs, n_stage, final, has_gain):
    x_ref = refs[0]
    stage_refs = refs[1:1 + 2 * n_stage]
    fw_ref = refs[1 + 2 * n_stage]
    n_cast_in = len(has_gain) + sum(has_gain)
    cast_in = refs[2 + 2 * n_stage:2 + 2 * n_stage + n_cast_in]
    o_ref = refs[2 + 2 * n_stage + n_cast_in]
    cast_out = refs[3 + 2 * n_stage + n_cast_in:3 + 2 * n_stage + n_cast_in + len(has_gain)]
    h_ref, act_ref = refs[3 + 2 * n_stage + n_cast_in + len(has_gain):]
    src_ref = x_ref
    for s in range(n_stage):
        wgu_ref, wd_ref = stage_refs[2 * s:2 * s + 2]
        x = src_ref[...]
        h_ref[...] = x.astype(BF16)
        rinv = _rms_scale(x)
        for c in range(D_FF // FF_CHUNK):
            if s == 0 and c == 2:
                _run_casts(cast_in, cast_out, has_gain)
            cols = slice(c * FF_CHUNK, (c + 1) * FF_CHUNK)
            a = rinv * jnp.dot(h_ref[...], wgu_ref[:, cols], preferred_element_type=F32)
            b = rinv * jnp.dot(h_ref[...], wgu_ref[:, D_FF + c * FF_CHUNK:D_FF + (c + 1) * FF_CHUNK],
                               preferred_element_type=F32)
            act_ref[:, cols] = (a * jax.nn.sigmoid(a) * b).astype(BF16)
        y = jnp.dot(act_ref[...], wd_ref[...], preferred_element_type=F32)
        out = src_ref[...] + 0.5 * y
        if final and s == n_stage - 1:
            out = _rms(out, fw_ref[...])
        o_ref[...] = out
        src_ref = o_ref


def _ffn_tile(stages):
    return 1024 if len(stages) == 1 else 512


def _ffn(x, stages, final_w, *, tm, final, casts=()):
    n = x.shape[0]
    in_specs = [pl.BlockSpec((tm, D_MODEL), lambda i: (i, 0))]
    args = [x]
    for w_gu, w_down in stages:
        in_specs += [_const_spec((D_MODEL, 2 * D_FF)), _const_spec((D_FF, D_MODEL))]
        args += [w_gu, w_down]
    in_specs.append(_const_spec((1, D_MODEL)))
    args.append(final_w)
    c_in, c_args, c_out, c_shapes, has_gain = _cast_specs(casts, n // tm, lambda i: i)
    return pl.pallas_call(
        functools.partial(_ffn_kernel, n_stage=len(stages), final=final, has_gain=has_gain),
        out_shape=[jax.ShapeDtypeStruct((n, D_MODEL), F32)] + c_shapes,
        grid=(n // tm,),
        in_specs=in_specs + c_in,
        out_specs=[pl.BlockSpec((tm, D_MODEL), lambda i: (i, 0))] + c_out,
        scratch_shapes=[pltpu.VMEM((tm, D_MODEL), BF16), pltpu.VMEM((tm, D_FF), BF16)],
        compiler_params=pltpu.CompilerParams(
            dimension_semantics=("arbitrary",), vmem_limit_bytes=VMEM_LIMIT_BYTES),
        name="ffn%d%s" % (len(stages), "_final" if final else ""),
    )(*args, *c_args)


def _table_kernel(freq_ref, logg_ref, cos_ref, sin_ref, dq_ref, dk_ref, bc_ref, bs_ref, *, pos0, period, seg):
    n = cos_ref.shape[0]
    pid = pl.program_id(0)

    @pl.when(pid == 0)
    def _():
        jb = jnp.bitwise_and(lax.broadcasted_iota(jnp.int32, (n, LANES), 0), period - 1).astype(F32)
        ang = jb * freq_ref[...]
        bc_ref[...] = jnp.cos(ang)
        bs_ref[...] = jnp.sin(ang)
        m = dq_ref.shape[0]
        j = jnp.bitwise_and(lax.broadcasted_iota(jnp.int32, (m, RET_WIDTH), 0), seg - 1).astype(F32)
        e = (j + 1.0) * logg_ref[...]
        dq_ref[...] = jnp.exp(e)
        dk_ref[...] = jnp.exp(-e) * (RET_KEY_DIM ** -0.5)

    off = (pos0 + pid * n).astype(F32) * freq_ref[...]
    co = jnp.cos(off)
    so = jnp.sin(off)
    bc = bc_ref[...]
    bs = bs_ref[...]
    cos_ref[...] = bc * co - bs * so
    s = bs * co + bc * so
    lane = lax.broadcasted_iota(jnp.int32, (n, LANES), 1)
    sin_ref[...] = jnp.where(lane < LANES // 2, -s, s)


def _tables(freq2, logg_lanes, *, n_pos, pos0, period, tm, seg):
    assert period & (period - 1) == 0 and seg & (seg - 1) == 0
    tb = min(n_pos, 512)
    assert period >= n_pos or n_pos == tb
    return pl.pallas_call(
        functools.partial(_table_kernel, pos0=pos0, period=period, seg=seg),
        out_shape=(
            jax.ShapeDtypeStruct((n_pos, LANES), F32),
            jax.ShapeDtypeStruct((n_pos, LANES), F32),
            jax.ShapeDtypeStruct((tm, RET_WIDTH), F32),
            jax.ShapeDtypeStruct((tm, RET_WIDTH), F32),
        ),
        grid=(n_pos // tb,),
        in_specs=[pl.BlockSpec((1, LANES), lambda i: (0, 0)),
                  pl.BlockSpec((1, RET_WIDTH), lambda i: (0, 0))],
        out_specs=(
            pl.BlockSpec((tb, LANES), lambda i: (i, 0)),
            pl.BlockSpec((tb, LANES), lambda i: (i, 0)),
            pl.BlockSpec((tm, RET_WIDTH), lambda i: (0, 0)),
            pl.BlockSpec((tm, RET_WIDTH), lambda i: (0, 0)),
        ),
        scratch_shapes=[pltpu.VMEM((tb, LANES), F32), pltpu.VMEM((tb, LANES), F32)],
        compiler_params=pltpu.CompilerParams(dimension_semantics=("arbitrary",)),
        name="tables",
    )(freq2, logg_lanes)


def _mix_kernel(*refs, tm, dense, seg, gseg, rows, carry, has_s0, emit_vn, has_gain):
    it = iter(refs)
    x_ref, cos_ref, sin_ref, dq_ref, dk_ref = (next(it) for _ in range(5))
    gnw_ref, lnw_ref, lnb_ref, ws_ref, bst_ref = (next(it) for _ in range(5))
    win_ref, wbr_ref, wbm_ref, wout_ref = (next(it) for _ in range(4))
    s0_ref = next(it) if has_s0 else None
    cast_in = [next(it) for _ in range(len(has_gain) + sum(has_gain))]
    o_ref, st_ref = next(it), next(it)
    vn_ref = next(it) if emit_vn else None
    cast_out = [next(it) for _ in has_gain]
    h_ref, qd_ref, kd_ref, v_ref, sg_ref, ret_ref, mrg_ref, u_ref, gat_ref, gm_ref = (next(it) for _ in range(10))
    if not emit_vn:
        vn_ref = next(it)

    if carry:
        @pl.when(pl.program_id(1) == 0)
        def _():
            st_ref[...] = s0_ref[...] if has_s0 else jnp.zeros(st_ref.shape, F32)
    else:
        st_ref[...] = s0_ref[...] if has_s0 else jnp.zeros(st_ref.shape, F32)

    x = x_ref[...]
    h_ref[...] = x.astype(BF16)
    rinv = _rms_scale(x)

    c_uv = 4 * RET_WIDTH
    c_ar = c_uv + 2 * GMLP_WIDTH
    c_am = c_ar + D_MODEL

    def proj(lo, hi):
        return rinv * jnp.dot(h_ref[...], win_ref[:, lo:hi], preferred_element_type=F32)

    qk = proj(0, 2 * RET_WIDTH)
    cos = cos_ref[...]
    sin = sin_ref[...]
    for hh in range(RET_HEADS):
        sl = slice(hh * LANES, (hh + 1) * LANES)
        q = qk[:, hh * LANES:(hh + 1) * LANES]
        k = qk[:, RET_WIDTH + hh * LANES:RET_WIDTH + (hh + 1) * LANES]
        q = q * cos + pltpu.roll(q, LANES // 2, 1) * sin
        k = k * cos + pltpu.roll(k, LANES // 2, 1) * sin
        qd_ref[:, sl] = (q * dq_ref[:, sl]).astype(BF16)
        kd_ref[:, sl] = (k * dk_ref[:, sl]).astype(BF16)

    vg = proj(2 * RET_WIDTH, 4 * RET_WIDTH)
    v_ref[...] = vg[:, :RET_WIDTH].astype(BF16)
    g = vg[:, RET_WIDTH:]
    sg_ref[...] = g * jax.nn.sigmoid(g)

    ri = lax.broadcasted_iota(jnp.int32, (dense, dense), 0)
    ci = lax.broadcasted_iota(jnp.int32, (dense, dense), 1)
    mask = ci <= ri
    if seg < dense:
        shift = seg.bit_length() - 1
        mask = mask & ((ri >> shift) == (ci >> shift))
    n_seg = dense // seg
    n_blk = tm // dense

    def retention_block(blk):
        r0 = blk * dense
        for hh in range(RET_HEADS):
            sl = slice(hh * LANES, (hh + 1) * LANES)
            qd = qd_ref[r0:r0 + dense, sl]
            kd = kd_ref[r0:r0 + dense, sl]
            vv = v_ref[r0:r0 + dense, sl]
            sc = lax.dot_general(qd, kd, (((1,), (1,)), ((), ())), preferred_element_type=F32)
            sc = jnp.where(mask, sc, 0.0).astype(BF16)
            intra = jnp.dot(sc, vv, preferred_element_type=F32)
            gam_seg = (1.0 - 2.0 ** (-5.0 - hh)) ** seg
            for sg_i in range(n_seg):
                a0 = sg_i * seg
                row = (blk * n_seg + sg_i) if rows > 1 else 0
                st = st_ref[row, hh]
                cross = jnp.dot(qd[a0:a0 + seg], st.astype(BF16), preferred_element_type=F32)
                ret_ref[r0 + a0:r0 + a0 + seg, sl] = intra[a0:a0 + seg] + cross
                kv = lax.dot_general(kd[a0:a0 + seg], vv[a0:a0 + seg], (((0,), (0,)), ((), ())),
                                     preferred_element_type=F32)
                st_ref[row, hh] = gam_seg * (st + kv)

    for blk in range((n_blk + 1) // 2):
        retention_block(blk)

    uv = proj(c_uv, c_ar)
    u_ref[...] = uv[:, :GMLP_WIDTH]
    vm = uv[:, GMLP_WIDTH:]
    vc = vm - jnp.mean(vm, axis=-1, keepdims=True)
    vn = vc * lax.rsqrt(jnp.mean(vc * vc, axis=-1, keepdims=True) + EPS)
    vn_ref[...] = vn * lnw_ref[...] + lnb_ref[...]

    for blk in range((n_blk + 1) // 2, n_blk):
        retention_block(blk)

    mrg_ref[...] = jax.nn.sigmoid(proj(c_ar, c_am))
    _run_casts(cast_in, cast_out, has_gain)

    for hh in range(RET_HEADS):
        sl = slice(hh * LANES, (hh + 1) * LANES)
        o = ret_ref[:, sl]
        oc = o - jnp.mean(o, axis=-1, keepdims=True)
        on = oc * lax.rsqrt(jnp.mean(oc * oc, axis=-1, keepdims=True) + EPS)
        gat_ref[:, sl] = (on * gnw_ref[:, sl] * sg_ref[:, sl]).astype(BF16)

    n_g = tm // gseg
    wi = lax.broadcasted_iota(jnp.int32, (gseg, gseg), 0)
    wj = lax.broadcasted_iota(jnp.int32, (gseg, gseg), 1)
    for gg in range(GMLP_GROUPS):
        sl = slice(gg * LANES, (gg + 1) * LANES)
        w = jnp.where(wj <= wi, ws_ref[gg, 0:gseg, 0:gseg], 0.0).astype(BF16)
        rhs = jnp.concatenate([vn_ref[c * gseg:(c + 1) * gseg, sl] for c in range(n_g)], axis=1)
        sp = jnp.dot(w, rhs.astype(BF16), preferred_element_type=F32)
        bias = bst_ref[0:gseg, gg:gg + 1]
        for c in range(n_g):
            s_c = sp[:, c * LANES:(c + 1) * LANES] + bias
            gm_ref[c * gseg:(c + 1) * gseg, sl] = (u_ref[c * gseg:(c + 1) * gseg, sl] * s_c).astype(BF16)

    g_mlp = jax.nn.sigmoid(proj(c_am, IN_COLS))
    y_ret = jnp.dot(gat_ref[...], wbr_ref[...], preferred_element_type=F32)
    y_mlp = jnp.dot(gm_ref[...], wbm_ref[...], preferred_element_type=F32)
    merged = (mrg_ref[...] * y_ret + g_mlp * y_mlp).astype(BF16)
    o_ref[...] = x_ref[...] + jnp.dot(merged, wout_ref[...], preferred_element_type=F32)


def _mix(x, tabs, gn_w, ln_w, ln_b, ws, bst, weights, s0, layer, *,
         tm, dense, seg, gseg, rows, n_outer, n_inner, carry, emit_vn, casts=()):
    w_in, w_br, w_bm, w_out = weights
    cos, sin, dq, dk = tabs
    has_s0 = s0 is not None
    n_tok = x.shape[0]
    n_state = n_outer * rows
    x_spec = pl.BlockSpec((tm, D_MODEL), lambda b, t: (b * n_inner + t, 0))
    if cos.shape[0] == tm:
        tab_spec = pl.BlockSpec((tm, LANES), lambda b, t: (0, 0))
    else:
        tab_spec = pl.BlockSpec((tm, LANES), lambda b, t: (t, 0))
    st_spec = pl.BlockSpec((rows, RET_HEADS, RET_KEY_DIM, RET_VAL_DIM), lambda b, t: (b, 0, 0, 0))
    in_specs = [
        x_spec, tab_spec, tab_spec,
        _const_spec((tm, RET_WIDTH)), _const_spec((tm, RET_WIDTH)),
        _const_spec((1, RET_WIDTH), layer),
        _const_spec((1, GMLP_WIDTH), layer), _const_spec((1, GMLP_WIDTH), layer),
        _const_spec((GMLP_GROUPS, GMLP_CHUNK, GMLP_CHUNK), layer),
        _const_spec((GMLP_CHUNK, GMLP_GROUPS), layer),
        _const_spec((D_MODEL, IN_COLS)), _const_spec((RET_WIDTH, D_MODEL)),
        _const_spec((GMLP_WIDTH, D_MODEL)), _const_spec((D_MODEL, D_MODEL)),
    ]
    args = [x, cos, sin, dq, dk, gn_w, ln_w, ln_b, ws, bst, w_in, w_br, w_bm, w_out]
    if has_s0:
        in_specs.append(pl.BlockSpec((None, rows, RET_HEADS, RET_KEY_DIM, RET_VAL_DIM),
                                     lambda b, t: (layer, b, 0, 0, 0)))
        args.append(s0)
    out_shape = [jax.ShapeDtypeStruct((n_tok, D_MODEL), F32),
                 jax.ShapeDtypeStruct((n_state, RET_HEADS, RET_KEY_DIM, RET_VAL_DIM), F32)]
    out_specs = [x_spec, st_spec]
    scratch = [
        pltpu.VMEM((tm, D_MODEL), BF16),
        pltpu.VMEM((tm, RET_WIDTH), BF16),
        pltpu.VMEM((tm, RET_WIDTH), BF16),
        pltpu.VMEM((tm, RET_WIDTH), BF16),
        pltpu.VMEM((tm, RET_WIDTH), F32),
        pltpu.VMEM((tm, RET_WIDTH), F32),
        pltpu.VMEM((tm, D_MODEL), F32),
        pltpu.VMEM((tm, GMLP_WIDTH), F32),
        pltpu.VMEM((tm, RET_WIDTH), BF16),
        pltpu.VMEM((tm, GMLP_WIDTH), BF16),
    ]
    if emit_vn:
        out_shape.append(jax.ShapeDtypeStruct((n_tok, GMLP_WIDTH), F32))
        out_specs.append(pl.BlockSpec((tm, GMLP_WIDTH), lambda b, t: (b * n_inner + t, 0)))
    else:
        scratch.append(pltpu.VMEM((tm, GMLP_WIDTH), F32))
    c_in, c_args, c_out, c_shapes, has_gain = _cast_specs(
        casts, n_outer * n_inner, lambda b, t: b * n_inner + t)
    return pl.pallas_call(
        functools.partial(_mix_kernel, tm=tm, dense=dense, seg=seg, gseg=gseg, rows=rows,
                          carry=carry, has_s0=has_s0, emit_vn=emit_vn, has_gain=has_gain),
        out_shape=out_shape + c_shapes,
        grid=(n_outer, n_inner),
        in_specs=in_specs + c_in,
        out_specs=out_specs + c_out,
        scratch_shapes=scratch,
        compiler_params=pltpu.CompilerParams(
            dimension_semantics=("arbitrary", "arbitrary"), vmem_limit_bytes=VMEM_LIMIT_BYTES),
        name="mix_rows" if rows > 1 else "mix_seq",
    )(*args, *c_args)


def kernel(x_prompt, x_sample, state_ret, ffn1_norm, ffn1_w_gu, ffn1_w_down, mix_norm, w_in, ret_gn_w, gmlp_ln_w, gmlp_ln_b, gmlp_ws, gmlp_bs, w_br_ret, w_br_mlp, w_out, ffn2_norm, ffn2_w_gu, ffn2_w_down, final_norm):
    bp, lp, _ = x_prompt.shape
    bs, ls, _ = x_sample.shape

    g1 = ffn1_norm[:, :, None]
    g2 = ffn2_norm[:, :, None]
    gm = mix_norm[:, :, None]
    stages = [((g1[0] * ffn1_w_gu[0]).astype(BF16), ffn1_w_down[0].astype(BF16))]
    gnw = ret_gn_w.reshape(DEPTH, 1, RET_WIDTH)
    lnw = gmlp_ln_w.reshape(DEPTH, 1, GMLP_WIDTH)
    lnb = gmlp_ln_b.reshape(DEPTH, 1, GMLP_WIDTH)
    bst = jnp.swapaxes(gmlp_bs, 1, 2)
    fw = final_norm.reshape(1, D_MODEL)

    half = RET_KEY_DIM // 2
    freqs = ROPE_BASE ** (-jnp.arange(half, dtype=F32) / half)
    freq2 = jnp.concatenate([freqs, freqs]).reshape(1, LANES)
    log_g = jnp.log1p(-jnp.exp2(-5.0 - jnp.arange(RET_HEADS, dtype=F32)))
    logg_lanes = jnp.repeat(log_g, LANES).reshape(1, RET_WIDTH)

    tm_p = 512
    dense_p = 256
    rows_s = 8
    tm_s = rows_s * ls
    tabs_p = _tables(freq2, logg_lanes, n_pos=lp, pos0=0, period=lp, tm=tm_p, seg=dense_p)
    tabs_s = _tables(freq2, logg_lanes, n_pos=tm_s, pos0=PAST_LEN, period=ls, tm=tm_s, seg=ls)

    xp = x_prompt.reshape(bp * lp, D_MODEL)
    xs = x_sample.reshape(bs * ls, D_MODEL)
    sp_list, ss_list, vs_list = [], [], []
    for l in range(DEPTH):
        mix_casts = [(w_in, l, gm), (w_br_ret, l, None), (w_br_mlp, l, None), (w_out, l, None)]
        xp, *mix_w = _ffn(xp, stages, fw, tm=_ffn_tile(stages), final=False, casts=mix_casts)
        xs, = _ffn(xs, stages, fw, tm=256, final=False)
        ffn_casts = [(ffn2_w_gu, l, g2), (ffn2_w_down, l, None)]
        if l + 1 < DEPTH:
            ffn_casts += [(ffn1_w_gu, l + 1, g1), (ffn1_w_down, l + 1, None)]
        xp, sp, *ffn_w = _mix(xp, tabs_p, gnw, lnw, lnb, gmlp_ws, bst, mix_w,
                              None, l, tm=tm_p, dense=dense_p, seg=dense_p, gseg=min(GMLP_CHUNK, lp), rows=1,
                              n_outer=bp, n_inner=lp // tm_p, carry=True, emit_vn=False, casts=ffn_casts)
        xs, ss, vs = _mix(xs, tabs_s, gnw, lnw, lnb, gmlp_ws, bst, mix_w,
                          state_ret, l, tm=tm_s, dense=tm_s, seg=ls, gseg=min(GMLP_CHUNK, ls), rows=rows_s,
                          n_outer=bs // rows_s, n_inner=1, carry=False, emit_vn=True)
        stages = [tuple(ffn_w[i:i + 2]) for i in range(0, len(ffn_w), 2)]
        sp_list.append(sp)
        ss_list.append(ss)
        vs_list.append(vs.reshape(bs, ls, GMLP_WIDTH))
    xp, = _ffn(xp, stages, fw, tm=_ffn_tile(stages), final=True)
    xs, = _ffn(xs, stages, fw, tm=256, final=True)
    y_prompt = xp.reshape(bp, lp, D_MODEL)
    y_sample = xs.reshape(bs, ls, D_MODEL)
    return (y_prompt, y_sample, jnp.stack(sp_list), jnp.stack(ss_list), jnp.stack(vs_list))
```

```python
import functools

import jax
import jax.numpy as jnp
from jax import lax
from jax.experimental import pallas as pl
from jax.experimental.pallas import tpu as pltpu

D_MODEL = 1024
DEPTH = 4
PAST_LEN = 4096
CHUNK = 64
RET_HEADS = 4
RET_KEY_DIM = 128
RET_VAL_DIM = 128
RET_WIDTH = RET_HEADS * RET_KEY_DIM
GMLP_CHUNK = 128
GMLP_GROUPS = 4
GMLP_WIDTH = 512
GMLP_GROUP_DIM = GMLP_WIDTH // GMLP_GROUPS
D_FF = 2816
ROPE_BASE = 10000.0
EPS = 1e-6
IN_COLS = 4 * RET_WIDTH + 2 * GMLP_WIDTH + 2 * D_MODEL

LANES = 128
FF_CHUNK = 256
VMEM_LIMIT_BYTES = 56 * 1024 * 1024

F32 = jnp.float32
BF16 = jnp.bfloat16


def _const_spec(shape, layer=None):
    nd = len(shape)
    if layer is None:
        return pl.BlockSpec(shape, lambda *_: (0,) * nd, pipeline_mode=pl.Buffered(1))
    return pl.BlockSpec((None,) + shape, lambda *_: (layer,) + (0,) * nd,
                        pipeline_mode=pl.Buffered(1))


BF16_SUBLANES = 16


def _cast_specs(casts, n_steps, linear_step):
    in_specs, args, out_specs, out_shapes, has_gain = [], [], [], [], []
    for src, layer, gain in casts:
        _, n_rows, n_cols = src.shape
        rb = BF16_SUBLANES
        while n_rows % rb or n_rows // rb > n_steps:
            rb += BF16_SUBLANES
        n_blk = n_rows // rb
        hold = n_steps // n_blk

        def blk(*g, hold=hold, n_blk=n_blk):
            return jnp.minimum(linear_step(*g) // hold, n_blk - 1)

        in_specs.append(pl.BlockSpec((None, rb, n_cols), lambda *g, blk=blk, layer=layer: (layer, blk(*g), 0)))
        args.append(src)
        if gain is not None:
            in_specs.append(pl.BlockSpec((None, rb, 1), lambda *g, blk=blk, layer=layer: (layer, blk(*g), 0)))
            args.append(gain)
        out_specs.append(pl.BlockSpec((rb, n_cols), lambda *g, blk=blk: (blk(*g), 0)))
        out_shapes.append(jax.ShapeDtypeStruct((n_rows, n_cols), BF16))
        has_gain.append(gain is not None)
    return in_specs, args, out_specs, out_shapes, tuple(has_gain)


def _run_casts(in_refs, out_refs, has_gain):
    it = iter(in_refs)
    for dst_ref, gained in zip(out_refs, has_gain):
        w = next(it)[...]
        if gained:
            w = w * next(it)[...]
        dst_ref[...] = w.astype(BF16)


def _rms_scale(x):
    return lax.rsqrt(jnp.mean(x * x, axis=-1, keepdims=True) + EPS)


def _rms(x, w):
    return x * _rms_scale(x) * w


def _ffn_kernel(*refs, n_stage, final, has_gain):
    x_ref = refs[0]
    stage_refs = refs[1:1 + 2 * n_stage]
    fw_ref = refs[1 + 2 * n_stage]
    n_cast_in = len(has_gain) + sum(has_gain)
    cast_in = refs[2 + 2 * n_stage:2 + 2 * n_stage + n_cast_in]
    o_ref = refs[2 + 2 * n_stage + n_cast_in]
    cast_out = refs[3 + 2 * n_stage + n_cast_in:3 + 2 * n_stage + n_cast_in + len(has_gain)]
    h_ref, act_ref = refs[3 + 2 * n_stage + n_cast_in + len(has_gain):]
    src_ref = x_ref
    for s in range(n_stage):
        wgu_ref, wd_ref = stage_refs[2 * s:2 * s + 2]
        x = src_ref[...]
        h_ref[...] = x.astype(BF16)
        rinv = _rms_scale(x)
        for c in range(D_FF // FF_CHUNK):
            if s == 0 and c == 2:
                _run_casts(cast_in, cast_out, has_gain)
            cols = slice(c * FF_CHUNK, (c + 1) * FF_CHUNK)
            a = rinv * jnp.dot(h_ref[...], wgu_ref[:, cols], preferred_element_type=F32)
            b = rinv * jnp.dot(h_ref[...], wgu_ref[:, D_FF + c * FF_CHUNK:D_FF + (c + 1) * FF_CHUNK],
                               preferred_element_type=F32)
            act_ref[:, cols] = (a * jax.nn.sigmoid(a) * b).astype(BF16)
        y = jnp.dot(act_ref[...], wd_ref[...], preferred_element_type=F32)
        out = src_ref[...] + 0.5 * y
        if final and s == n_stage - 1:
            out = _rms(out, fw_ref[...])
        o_ref[...] = out
        src_ref = o_ref


def _ffn_tile(stages):
    return 1024 if len(stages) == 1 else 512


def _ffn(x, stages, final_w, *, tm, final, casts=()):
    n = x.shape[0]
    in_specs = [pl.BlockSpec((tm, D_MODEL), lambda i: (i, 0))]
    args = [x]
    for w_gu, w_down in stages:
        in_specs += [_const_spec((D_MODEL, 2 * D_FF)), _const_spec((D_FF, D_MODEL))]
        args += [w_gu, w_down]
    in_specs.append(_const_spec((1, D_MODEL)))
    args.append(final_w)
    c_in, c_args, c_out, c_shapes, has_gain = _cast_specs(casts, n // tm, lambda i: i)
    return pl.pallas_call(
        functools.partial(_ffn_kernel, n_stage=len(stages), final=final, has_gain=has_gain),
        out_shape=[jax.ShapeDtypeStruct((n, D_MODEL), F32)] + c_shapes,
        grid=(n // tm,),
        in_specs=in_specs + c_in,
        out_specs=[pl.BlockSpec((tm, D_MODEL), lambda i: (i, 0))] + c_out,
        scratch_shapes=[pltpu.VMEM((tm, D_MODEL), BF16), pltpu.VMEM((tm, D_FF), BF16)],
        compiler_params=pltpu.CompilerParams(
            dimension_semantics=("arbitrary",), vmem_limit_bytes=VMEM_LIMIT_BYTES),
        name="ffn%d%s" % (len(stages), "_final" if final else ""),
    )(*args, *c_args)


def _table_kernel(freq_ref, logg_ref, cos_ref, sin_ref, dq_ref, dk_ref, bc_ref, bs_ref, *, pos0, period, seg):
    n = cos_ref.shape[0]
    pid = pl.program_id(0)

    @pl.when(pid == 0)
    def _():
        jb = jnp.bitwise_and(lax.broadcasted_iota(jnp.int32, (n, LANES), 0), period - 1).astype(F32)
        ang = jb * freq_ref[...]
        bc_ref[...] = jnp.cos(ang)
        bs_ref[...] = jnp.sin(ang)
        m = dq_ref.shape[0]
        j = jnp.bitwise_and(lax.broadcasted_iota(jnp.int32, (m, RET_WIDTH), 0), seg - 1).astype(F32)
        e = (j + 1.0) * logg_ref[...]
        dq_ref[...] = jnp.exp(e)
        dk_ref[...] = jnp.exp(-e) * (RET_KEY_DIM ** -0.5)

    off = (pos0 + pid * n).astype(F32) * freq_ref[...]
    co = jnp.cos(off)
    so = jnp.sin(off)
    bc = bc_ref[...]
    bs = bs_ref[...]
    cos_ref[...] = bc * co - bs * so
    s = bs * co + bc * so
    lane = lax.broadcasted_iota(jnp.int32, (n, LANES), 1)
    sin_ref[...] = jnp.where(lane < LANES // 2, -s, s)


def _tables(freq2, logg_lanes, *, n_pos, pos0, period, tm, seg):
    assert period & (period - 1) == 0 and seg & (seg - 1) == 0
    tb = min(n_pos, 512)
    assert period >= n_pos or n_pos == tb
    return pl.pallas_call(
        functools.partial(_table_kernel, pos0=pos0, period=period, seg=seg),
        out_shape=(
            jax.ShapeDtypeStruct((n_pos, LANES), F32),
            jax.ShapeDtypeStruct((n_pos, LANES), F32),
            jax.ShapeDtypeStruct((tm, RET_WIDTH), F32),
            jax.ShapeDtypeStruct((tm, RET_WIDTH), F32),
        ),
        grid=(n_pos // tb,),
        in_specs=[pl.BlockSpec((1, LANES), lambda i: (0, 0)),
                  pl.BlockSpec((1, RET_WIDTH), lambda i: (0, 0))],
        out_specs=(
            pl.BlockSpec((tb, LANES), lambda i: (i, 0)),
            pl.BlockSpec((tb, LANES), lambda i: (i, 0)),
            pl.BlockSpec((tm, RET_WIDTH), lambda i: (0, 0)),
            pl.BlockSpec((tm, RET_WIDTH), lambda i: (0, 0)),
        ),
        scratch_shapes=[pltpu.VMEM((tb, LANES), F32), pltpu.VMEM((tb, LANES), F32)],
        compiler_params=pltpu.CompilerParams(dimension_semantics=("arbitrary",)),
        name="tables",
    )(freq2, logg_lanes)


def _mix_kernel(*refs, tm, parts, dense, seg, gseg, rows, carry, has_s0, emit_vn, has_gain):
    it = iter(refs)
    x_ref, cos_ref, sin_ref, dq_ref, dk_ref = (next(it) for _ in range(5))
    gnw_ref, lnw_ref, lnb_ref, ws_ref, bst_ref = (next(it) for _ in range(5))
    win_ref, wbr_ref, wbm_ref, wout_ref = (next(it) for _ in range(4))
    s0_ref = next(it) if has_s0 else None
    cast_in = [next(it) for _ in range(len(has_gain) + sum(has_gain))]
    o_ref, st_ref = next(it), next(it)
    vn_ref = next(it) if emit_vn else None
    cast_out = [next(it) for _ in has_gain]
    scratch = [next(it) for _ in range(10)]
    vn_scratch = None if emit_vn else next(it)

    if carry:
        @pl.when(pl.program_id(1) == 0)
        def _():
            st_ref[...] = s0_ref[...] if has_s0 else jnp.zeros(st_ref.shape, F32)
    else:
        st_ref[...] = s0_ref[...] if has_s0 else jnp.zeros(st_ref.shape, F32)

    for part in range(parts):
        rs = slice(part * tm, (part + 1) * tm)
        casts = functools.partial(_run_casts, cast_in, cast_out, has_gain) if part == 0 else None
        _mix_tile(x_ref.at[rs], cos_ref.at[rs], sin_ref.at[rs], dq_ref, dk_ref,
                  gnw_ref, lnw_ref, lnb_ref, ws_ref, bst_ref, win_ref, wbr_ref, wbm_ref, wout_ref,
                  o_ref.at[rs], st_ref, vn_ref.at[rs] if emit_vn else vn_scratch, scratch, casts,
                  tm=tm, dense=dense, seg=seg, gseg=gseg, rows=rows)


def _mix_tile(x_ref, cos_ref, sin_ref, dq_ref, dk_ref, gnw_ref, lnw_ref, lnb_ref, ws_ref, bst_ref,
              win_ref, wbr_ref, wbm_ref, wout_ref, o_ref, st_ref, vn_ref, scratch, casts,
              *, tm, dense, seg, gseg, rows):
    h_ref, qd_ref, kd_ref, v_ref, sg_ref, ret_ref, mrg_ref, u_ref, gat_ref, gm_ref = scratch

    x = x_ref[...]
    h_ref[...] = x.astype(BF16)
    rinv = _rms_scale(x)

    c_uv = 4 * RET_WIDTH
    c_ar = c_uv + 2 * GMLP_WIDTH
    c_am = c_ar + D_MODEL

    def proj(lo, hi):
        return rinv * jnp.dot(h_ref[...], win_ref[:, lo:hi], preferred_element_type=F32)

    qk = proj(0, 2 * RET_WIDTH)
    cos = cos_ref[...]
    sin = sin_ref[...]
    for hh in range(RET_HEADS):
        sl = slice(hh * LANES, (hh + 1) * LANES)
        q = qk[:, hh * LANES:(hh + 1) * LANES]
        k = qk[:, RET_WIDTH + hh * LANES:RET_WIDTH + (hh + 1) * LANES]
        q = q * cos + pltpu.roll(q, LANES // 2, 1) * sin
        k = k * cos + pltpu.roll(k, LANES // 2, 1) * sin
        qd_ref[:, sl] = (q * dq_ref[:, sl]).astype(BF16)
        kd_ref[:, sl] = (k * dk_ref[:, sl]).astype(BF16)

    vg = proj(2 * RET_WIDTH, 4 * RET_WIDTH)
    v_ref[...] = vg[:, :RET_WIDTH].astype(BF16)
    g = vg[:, RET_WIDTH:]
    sg_ref[...] = g * jax.nn.sigmoid(g)

    ri = lax.broadcasted_iota(jnp.int32, (dense, dense), 0)
    ci = lax.broadcasted_iota(jnp.int32, (dense, dense), 1)
    mask = ci <= ri
    if seg < dense:
        shift = seg.bit_length() - 1
        mask = mask & ((ri >> shift) == (ci >> shift))
    n_seg = dense // seg
    n_blk = tm // dense

    def retention_block(blk):
        r0 = blk * dense
        for hh in range(RET_HEADS):
            sl = slice(hh * LANES, (hh + 1) * LANES)
            qd = qd_ref[r0:r0 + dense, sl]
            kd = kd_ref[r0:r0 + dense, sl]
            vv = v_ref[r0:r0 + dense, sl]
            sc = lax.dot_general(qd, kd, (((1,), (1,)), ((), ())), preferred_element_type=F32)
            sc = jnp.where(mask, sc, 0.0).astype(BF16)
            intra = jnp.dot(sc, vv, preferred_element_type=F32)
            gam_seg = (1.0 - 2.0 ** (-5.0 - hh)) ** seg
            for sg_i in range(n_seg):
                a0 = sg_i * seg
                row = (blk * n_seg + sg_i) if rows > 1 else 0
                st = st_ref[row, hh]
                cross = jnp.dot(qd[a0:a0 + seg], st.astype(BF16), preferred_element_type=F32)
                ret_ref[r0 + a0:r0 + a0 + seg, sl] = intra[a0:a0 + seg] + cross
                kv = lax.dot_general(kd[a0:a0 + seg], vv[a0:a0 + seg], (((0,), (0,)), ((), ())),
                                     preferred_element_type=F32)
                st_ref[row, hh] = gam_seg * (st + kv)

    for blk in range((n_blk + 1) // 2):
        retention_block(blk)

    uv = proj(c_uv, c_ar)
    u_ref[...] = uv[:, :GMLP_WIDTH]
    vm = uv[:, GMLP_WIDTH:]
    vc = vm - jnp.mean(vm, axis=-1, keepdims=True)
    vn = vc * lax.rsqrt(jnp.mean(vc * vc, axis=-1, keepdims=True) + EPS)
    vn_ref[...] = vn * lnw_ref[...] + lnb_ref[...]

    for blk in range((n_blk + 1) // 2, n_blk):
        retention_block(blk)

    mrg_ref[...] = jax.nn.sigmoid(proj(c_ar, c_am))
    if casts is not None:
        casts()

    for hh in range(RET_HEADS):
        sl = slice(hh * LANES, (hh + 1) * LANES)
        o = ret_ref[:, sl]
        oc = o - jnp.mean(o, axis=-1, keepdims=True)
        on = oc * lax.rsqrt(jnp.mean(oc * oc, axis=-1, keepdims=True) + EPS)
        gat_ref[:, sl] = (on * gnw_ref[:, sl] * sg_ref[:, sl]).astype(BF16)

    n_g = tm // gseg
    wi = lax.broadcasted_iota(jnp.int32, (gseg, gseg), 0)
    wj = lax.broadcasted_iota(jnp.int32, (gseg, gseg), 1)
    for gg in range(GMLP_GROUPS):
        sl = slice(gg * LANES, (gg + 1) * LANES)
        w = jnp.where(wj <= wi, ws_ref[gg, 0:gseg, 0:gseg], 0.0).astype(BF16)
        rhs = jnp.concatenate([vn_ref[c * gseg:(c + 1) * gseg, sl] for c in range(n_g)], axis=1)
        sp = jnp.dot(w, rhs.astype(BF16), preferred_element_type=F32)
        bias = bst_ref[0:gseg, gg:gg + 1]
        for c in range(n_g):
            s_c = sp[:, c * LANES:(c + 1) * LANES] + bias
            gm_ref[c * gseg:(c + 1) * gseg, sl] = (u_ref[c * gseg:(c + 1) * gseg, sl] * s_c).astype(BF16)

    g_mlp = jax.nn.sigmoid(proj(c_am, IN_COLS))
    y_ret = jnp.dot(gat_ref[...], wbr_ref[...], preferred_element_type=F32)
    y_mlp = jnp.dot(gm_ref[...], wbm_ref[...], preferred_element_type=F32)
    merged = (mrg_ref[...] * y_ret + g_mlp * y_mlp).astype(BF16)
    o_ref[...] = x_ref[...] + jnp.dot(merged, wout_ref[...], preferred_element_type=F32)


def _mix(x, tabs, gn_w, ln_w, ln_b, ws, bst, weights, s0, layer, *,
         tm, parts, dense, seg, gseg, rows, n_outer, n_inner, carry, emit_vn, casts=()):
    w_in, w_br, w_bm, w_out = weights
    cos, sin, dq, dk = tabs
    has_s0 = s0 is not None
    n_tok = x.shape[0]
    n_state = n_outer * rows
    tb = parts * tm
    x_spec = pl.BlockSpec((tb, D_MODEL), lambda b, t: (b * n_inner + t, 0))
    if cos.shape[0] == tb:
        tab_spec = pl.BlockSpec((tb, LANES), lambda b, t: (0, 0))
    else:
        tab_spec = pl.BlockSpec((tb, LANES), lambda b, t: (t, 0))
    st_spec = pl.BlockSpec((rows, RET_HEADS, RET_KEY_DIM, RET_VAL_DIM), lambda b, t: (b, 0, 0, 0))
    in_specs = [
        x_spec, tab_spec, tab_spec,
        _const_spec((tm, RET_WIDTH)), _const_spec((tm, RET_WIDTH)),
        _const_spec((1, RET_WIDTH), layer),
        _const_spec((1, GMLP_WIDTH), layer), _const_spec((1, GMLP_WIDTH), layer),
        _const_spec((GMLP_GROUPS, GMLP_CHUNK, GMLP_CHUNK), layer),
        _const_spec((GMLP_CHUNK, GMLP_GROUPS), layer),
        _const_spec((D_MODEL, IN_COLS)), _const_spec((RET_WIDTH, D_MODEL)),
        _const_spec((GMLP_WIDTH, D_MODEL)), _const_spec((D_MODEL, D_MODEL)),
    ]
    args = [x, cos, sin, dq, dk, gn_w, ln_w, ln_b, ws, bst, w_in, w_br, w_bm, w_out]
    if has_s0:
        in_specs.append(pl.BlockSpec((None, rows, RET_HEADS, RET_KEY_DIM, RET_VAL_DIM),
                                     lambda b, t: (layer, b, 0, 0, 0)))
        args.append(s0)
    out_shape = [jax.ShapeDtypeStruct((n_tok, D_MODEL), F32),
                 jax.ShapeDtypeStruct((n_state, RET_HEADS, RET_KEY_DIM, RET_VAL_DIM), F32)]
    out_specs = [x_spec, st_spec]
    scratch = [
        pltpu.VMEM((tm, D_MODEL), BF16),
        pltpu.VMEM((tm, RET_WIDTH), BF16),
        pltpu.VMEM((tm, RET_WIDTH), BF16),
        pltpu.VMEM((tm, RET_WIDTH), BF16),
        pltpu.VMEM((tm, RET_WIDTH), F32),
        pltpu.VMEM((tm, RET_WIDTH), F32),
        pltpu.VMEM((tm, D_MODEL), F32),
        pltpu.VMEM((tm, GMLP_WIDTH), F32),
        pltpu.VMEM((tm, RET_WIDTH), BF16),
        pltpu.VMEM((tm, GMLP_WIDTH), BF16),
    ]
    if emit_vn:
        out_shape.append(jax.ShapeDtypeStruct((n_tok, GMLP_WIDTH), F32))
        out_specs.append(pl.BlockSpec((tb, GMLP_WIDTH), lambda b, t: (b * n_inner + t, 0)))
    else:
        scratch.append(pltpu.VMEM((tm, GMLP_WIDTH), F32))
    c_in, c_args, c_out, c_shapes, has_gain = _cast_specs(
        casts, n_outer * n_inner, lambda b, t: b * n_inner + t)
    return pl.pallas_call(
        functools.partial(_mix_kernel, tm=tm, parts=parts, dense=dense, seg=seg, gseg=gseg, rows=rows,
                          carry=carry, has_s0=has_s0, emit_vn=emit_vn, has_gain=has_gain),
        out_shape=out_shape + c_shapes,
        grid=(n_outer, n_inner),
        in_specs=in_specs + c_in,
        out_specs=out_specs + c_out,
        scratch_shapes=scratch,
        compiler_params=pltpu.CompilerParams(
            dimension_semantics=("arbitrary", "arbitrary"), vmem_limit_bytes=VMEM_LIMIT_BYTES),
        name="mix_rows" if rows > 1 else "mix_seq",
    )(*args, *c_args)


def kernel(x_prompt, x_sample, state_ret, ffn1_norm, ffn1_w_gu, ffn1_w_down, mix_norm, w_in, ret_gn_w, gmlp_ln_w, gmlp_ln_b, gmlp_ws, gmlp_bs, w_br_ret, w_br_mlp, w_out, ffn2_norm, ffn2_w_gu, ffn2_w_down, final_norm):
    bp, lp, _ = x_prompt.shape
    bs, ls, _ = x_sample.shape

    g1 = ffn1_norm[:, :, None]
    g2 = ffn2_norm[:, :, None]
    gm = mix_norm[:, :, None]
    stages = [((g1[0] * ffn1_w_gu[0]).astype(BF16), ffn1_w_down[0].astype(BF16))]
    gnw = ret_gn_w.reshape(DEPTH, 1, RET_WIDTH)
    lnw = gmlp_ln_w.reshape(DEPTH, 1, GMLP_WIDTH)
    lnb = gmlp_ln_b.reshape(DEPTH, 1, GMLP_WIDTH)
    bst = jnp.swapaxes(gmlp_bs, 1, 2)
    fw = final_norm.reshape(1, D_MODEL)

    half = RET_KEY_DIM // 2
    freqs = ROPE_BASE ** (-jnp.arange(half, dtype=F32) / half)
    freq2 = jnp.concatenate([freqs, freqs]).reshape(1, LANES)
    log_g = jnp.log1p(-jnp.exp2(-5.0 - jnp.arange(RET_HEADS, dtype=F32)))
    logg_lanes = jnp.repeat(log_g, LANES).reshape(1, RET_WIDTH)

    tm_p = 512
    parts_p = 2
    dense_p = 256
    rows_s = 8
    tm_s = rows_s * ls
    tabs_p = _tables(freq2, logg_lanes, n_pos=lp, pos0=0, period=lp, tm=tm_p, seg=dense_p)
    tabs_s = _tables(freq2, logg_lanes, n_pos=tm_s, pos0=PAST_LEN, period=ls, tm=tm_s, seg=ls)

    xp = x_prompt.reshape(bp * lp, D_MODEL)
    xs = x_sample.reshape(bs * ls, D_MODEL)
    sp_list, ss_list, vs_list = [], [], []
    for l in range(DEPTH):
        mix_casts = [(w_in, l, gm), (w_br_ret, l, None), (w_br_mlp, l, None), (w_out, l, None)]
        xp, *mix_w = _ffn(xp, stages, fw, tm=_ffn_tile(stages), final=False, casts=mix_casts)
        xs, = _ffn(xs, stages, fw, tm=256, final=False)
        ffn_casts = [(ffn2_w_gu, l, g2), (ffn2_w_down, l, None)]
        if l + 1 < DEPTH:
            ffn_casts += [(ffn1_w_gu, l + 1, g1), (ffn1_w_down, l + 1, None)]
        xp, sp, *ffn_w = _mix(xp, tabs_p, gnw, lnw, lnb, gmlp_ws, bst, mix_w,
                              None, l, tm=tm_p, parts=parts_p, dense=dense_p, seg=dense_p,
                              gseg=min(GMLP_CHUNK, lp), rows=1, n_outer=bp, n_inner=lp // (parts_p * tm_p),
                              carry=True, emit_vn=False, casts=ffn_casts)
        xs, ss, vs = _mix(xs, tabs_s, gnw, lnw, lnb, gmlp_ws, bst, mix_w,
                          state_ret, l, tm=tm_s, parts=1, dense=tm_s, seg=ls, gseg=min(GMLP_CHUNK, ls), rows=rows_s,
                          n_outer=bs // rows_s, n_inner=1, carry=False, emit_vn=True)
        stages = [tuple(ffn_w[i:i + 2]) for i in range(0, len(ffn_w), 2)]
        sp_list.append(sp)
        ss_list.append(ss)
        vs_list.append(vs.reshape(bs, ls, GMLP_WIDTH))
    xp, = _ffn(xp, stages, fw, tm=_ffn_tile(stages), final=True)
    xs, = _ffn(xs, stages, fw, tm=256, final=True)
    y_prompt = xp.reshape(bp, lp, D_MODEL)
    y_sample = xs.reshape(bs, ls, D_MODEL)
    return (y_prompt, y_sample, jnp.stack(sp_list), jnp.stack(ss_list), jnp.stack(vs_list))
```

```python
import functools

import jax
import jax.numpy as jnp
from jax import lax
from jax.experimental import pallas as pl
from jax.experimental.pallas import tpu as pltpu

D_MODEL = 1024
DEPTH = 4
PAST_LEN = 4096
CHUNK = 64
RET_HEADS = 4
RET_KEY_DIM = 128
RET_VAL_DIM = 128
RET_WIDTH = RET_HEADS * RET_KEY_DIM
GMLP_CHUNK = 128
GMLP_GROUPS = 4
GMLP_WIDTH = 512
GMLP_GROUP_DIM = GMLP_WIDTH // GMLP_GROUPS
D_FF = 2816
ROPE_BASE = 10000.0
EPS = 1e-6
IN_COLS = 4 * RET_WIDTH + 2 * GMLP_WIDTH + 2 * D_MODEL

LANES = 128
FF_CHUNK = 256
VMEM_LIMIT_BYTES = 56 * 1024 * 1024

F32 = jnp.float32
BF16 = jnp.bfloat16


def _const_spec(shape, layer=None):
    nd = len(shape)
    if layer is None:
        return pl.BlockSpec(shape, lambda *_: (0,) * nd, pipeline_mode=pl.Buffered(1))
    return pl.BlockSpec((None,) + shape, lambda *_: (layer,) + (0,) * nd,
                        pipeline_mode=pl.Buffered(1))


BF16_SUBLANES = 16


def _cast_specs(casts, n_steps, linear_step):
    in_specs, args, out_specs, out_shapes, has_gain = [], [], [], [], []
    for src, layer, gain in casts:
        _, n_rows, n_cols = src.shape
        rb = BF16_SUBLANES
        while n_rows % rb or n_rows // rb > n_steps:
            rb += BF16_SUBLANES
        n_blk = n_rows // rb
        hold = n_steps // n_blk

        def blk(*g, hold=hold, n_blk=n_blk):
            return jnp.minimum(linear_step(*g) // hold, n_blk - 1)

        in_specs.append(pl.BlockSpec((None, rb, n_cols), lambda *g, blk=blk, layer=layer: (layer, blk(*g), 0)))
        args.append(src)
        if gain is not None:
            in_specs.append(pl.BlockSpec((None, rb, 1), lambda *g, blk=blk, layer=layer: (layer, blk(*g), 0)))
            args.append(gain)
        out_specs.append(pl.BlockSpec((rb, n_cols), lambda *g, blk=blk: (blk(*g), 0)))
        out_shapes.append(jax.ShapeDtypeStruct((n_rows, n_cols), BF16))
        has_gain.append(gain is not None)
    return in_specs, args, out_specs, out_shapes, tuple(has_gain)


def _run_casts(in_refs, out_refs, has_gain):
    it = iter(in_refs)
    for dst_ref, gained in zip(out_refs, has_gain):
        w = next(it)[...]
        if gained:
            w = w * next(it)[...]
        dst_ref[...] = w.astype(BF16)


def _rms_scale(x):
    return lax.rsqrt(jnp.mean(x * x, axis=-1, keepdims=True) + EPS)


def _rms(x, w):
    return x * _rms_scale(x) * w


def _sigmoid(x):
    return 0.5 * jnp.tanh(0.5 * x) + 0.5


def _silu(x):
    h = 0.5 * x
    return h * jnp.tanh(h) + h


def _ffn_kernel(*refs, n_stage, final, has_gain):
    x_ref = refs[0]
    stage_refs = refs[1:1 + 2 * n_stage]
    fw_ref = refs[1 + 2 * n_stage]
    n_cast_in = len(has_gain) + sum(has_gain)
    cast_in = refs[2 + 2 * n_stage:2 + 2 * n_stage + n_cast_in]
    o_ref = refs[2 + 2 * n_stage + n_cast_in]
    cast_out = refs[3 + 2 * n_stage + n_cast_in:3 + 2 * n_stage + n_cast_in + len(has_gain)]
    h_ref, act_ref = refs[3 + 2 * n_stage + n_cast_in + len(has_gain):]
    src_ref = x_ref
    for s in range(n_stage):
        wgu_ref, wd_ref = stage_refs[2 * s:2 * s + 2]
        x = src_ref[...]
        h_ref[...] = x.astype(BF16)
        rinv = _rms_scale(x)
        for c in range(D_FF // FF_CHUNK):
            if s == 0 and c == 2:
                _run_casts(cast_in, cast_out, has_gain)
            cols = slice(c * FF_CHUNK, (c + 1) * FF_CHUNK)
            a = rinv * jnp.dot(h_ref[...], wgu_ref[:, cols], preferred_element_type=F32)
            b = rinv * jnp.dot(h_ref[...], wgu_ref[:, D_FF + c * FF_CHUNK:D_FF + (c + 1) * FF_CHUNK],
                               preferred_element_type=F32)
            act_ref[:, cols] = (_silu(a) * b).astype(BF16)
        y = jnp.dot(act_ref[...], wd_ref[...], preferred_element_type=F32)
        out = src_ref[...] + 0.5 * y
        if final and s == n_stage - 1:
            out = _rms(out, fw_ref[...])
        o_ref[...] = out
        src_ref = o_ref


def _ffn_tile(stages):
    return 1024 if len(stages) == 1 else 512


def _ffn(x, stages, final_w, *, tm, final, casts=()):
    n = x.shape[0]
    in_specs = [pl.BlockSpec((tm, D_MODEL), lambda i: (i, 0))]
    args = [x]
    for w_gu, w_down in stages:
        in_specs += [_const_spec((D_MODEL, 2 * D_FF)), _const_spec((D_FF, D_MODEL))]
        args += [w_gu, w_down]
    in_specs.append(_const_spec((1, D_MODEL)))
    args.append(final_w)
    c_in, c_args, c_out, c_shapes, has_gain = _cast_specs(casts, n // tm, lambda i: i)
    return pl.pallas_call(
        functools.partial(_ffn_kernel, n_stage=len(stages), final=final, has_gain=has_gain),
        out_shape=[jax.ShapeDtypeStruct((n, D_MODEL), F32)] + c_shapes,
        grid=(n // tm,),
        in_specs=in_specs + c_in,
        out_specs=[pl.BlockSpec((tm, D_MODEL), lambda i: (i, 0))] + c_out,
        scratch_shapes=[pltpu.VMEM((tm, D_MODEL), BF16), pltpu.VMEM((tm, D_FF), BF16)],
        compiler_params=pltpu.CompilerParams(
            dimension_semantics=("arbitrary",), vmem_limit_bytes=VMEM_LIMIT_BYTES),
        name="ffn%d%s" % (len(stages), "_final" if final else ""),
    )(*args, *c_args)


def _table_kernel(freq_ref, logg_ref, cos_ref, sin_ref, dq_ref, dk_ref, bc_ref, bs_ref, *, pos0, period, seg):
    n = cos_ref.shape[0]
    pid = pl.program_id(0)

    @pl.when(pid == 0)
    def _():
        jb = jnp.bitwise_and(lax.broadcasted_iota(jnp.int32, (n, LANES), 0), period - 1).astype(F32)
        ang = jb * freq_ref[...]
        bc_ref[...] = jnp.cos(ang)
        bs_ref[...] = jnp.sin(ang)
        m = dq_ref.shape[0]
        j = jnp.bitwise_and(lax.broadcasted_iota(jnp.int32, (m, RET_WIDTH), 0), seg - 1).astype(F32)
        e = (j + 1.0) * logg_ref[...]
        dq_ref[...] = jnp.exp(e)
        dk_ref[...] = jnp.exp(-e) * (RET_KEY_DIM ** -0.5)

    off = (pos0 + pid * n).astype(F32) * freq_ref[...]
    co = jnp.cos(off)
    so = jnp.sin(off)
    bc = bc_ref[...]
    bs = bs_ref[...]
    cos_ref[...] = bc * co - bs * so
    s = bs * co + bc * so
    lane = lax.broadcasted_iota(jnp.int32, (n, LANES), 1)
    sin_ref[...] = jnp.where(lane < LANES // 2, -s, s)


def _tables(freq2, logg_lanes, *, n_pos, pos0, period, tm, seg):
    assert period & (period - 1) == 0 and seg & (seg - 1) == 0
    tb = min(n_pos, 512)
    assert period >= n_pos or n_pos == tb
    return pl.pallas_call(
        functools.partial(_table_kernel, pos0=pos0, period=period, seg=seg),
        out_shape=(
            jax.ShapeDtypeStruct((n_pos, LANES), F32),
            jax.ShapeDtypeStruct((n_pos, LANES), F32),
            jax.ShapeDtypeStruct((tm, RET_WIDTH), F32),
            jax.ShapeDtypeStruct((tm, RET_WIDTH), F32),
        ),
        grid=(n_pos // tb,),
        in_specs=[pl.BlockSpec((1, LANES), lambda i: (0, 0)),
                  pl.BlockSpec((1, RET_WIDTH), lambda i: (0, 0))],
        out_specs=(
            pl.BlockSpec((tb, LANES), lambda i: (i, 0)),
            pl.BlockSpec((tb, LANES), lambda i: (i, 0)),
            pl.BlockSpec((tm, RET_WIDTH), lambda i: (0, 0)),
            pl.BlockSpec((tm, RET_WIDTH), lambda i: (0, 0)),
        ),
        scratch_shapes=[pltpu.VMEM((tb, LANES), F32), pltpu.VMEM((tb, LANES), F32)],
        compiler_params=pltpu.CompilerParams(dimension_semantics=("arbitrary",)),
        name="tables",
    )(freq2, logg_lanes)


def _mix_kernel(*refs, tm, parts, dense, seg, gseg, rows, carry, has_s0, emit_vn, has_gain):
    it = iter(refs)
    x_ref, cos_ref, sin_ref, dq_ref, dk_ref = (next(it) for _ in range(5))
    gnw_ref, lnw_ref, lnb_ref, ws_ref, bst_ref = (next(it) for _ in range(5))
    win_ref, wbr_ref, wbm_ref, wout_ref = (next(it) for _ in range(4))
    s0_ref = next(it) if has_s0 else None
    cast_in = [next(it) for _ in range(len(has_gain) + sum(has_gain))]
    o_ref, st_ref = next(it), next(it)
    vn_ref = next(it) if emit_vn else None
    cast_out = [next(it) for _ in has_gain]
    scratch = [next(it) for _ in range(10)]
    vn_scratch = None if emit_vn else next(it)

    if carry:
        @pl.when(pl.program_id(1) == 0)
        def _():
            st_ref[...] = s0_ref[...] if has_s0 else jnp.zeros(st_ref.shape, F32)
    else:
        st_ref[...] = s0_ref[...] if has_s0 else jnp.zeros(st_ref.shape, F32)

    for part in range(parts):
        rs = slice(part * tm, (part + 1) * tm)
        casts = functools.partial(_run_casts, cast_in, cast_out, has_gain) if part == 0 else None
        _mix_tile(x_ref.at[rs], cos_ref.at[rs], sin_ref.at[rs], dq_ref, dk_ref,
                  gnw_ref, lnw_ref, lnb_ref, ws_ref, bst_ref, win_ref, wbr_ref, wbm_ref, wout_ref,
                  o_ref.at[rs], st_ref, vn_ref.at[rs] if emit_vn else vn_scratch, scratch, casts,
                  tm=tm, dense=dense, seg=seg, gseg=gseg, rows=rows)


def _mix_tile(x_ref, cos_ref, sin_ref, dq_ref, dk_ref, gnw_ref, lnw_ref, lnb_ref, ws_ref, bst_ref,
              win_ref, wbr_ref, wbm_ref, wout_ref, o_ref, st_ref, vn_ref, scratch, casts,
              *, tm, dense, seg, gseg, rows):
    h_ref, qd_ref, kd_ref, v_ref, sg_ref, ret_ref, mrg_ref, u_ref, gat_ref, gm_ref = scratch

    x = x_ref[...]
    h_ref[...] = x.astype(BF16)
    rinv = _rms_scale(x)

    c_uv = 4 * RET_WIDTH
    c_ar = c_uv + 2 * GMLP_WIDTH
    c_am = c_ar + D_MODEL

    def proj(lo, hi):
        return rinv * jnp.dot(h_ref[...], win_ref[:, lo:hi], preferred_element_type=F32)

    qk = proj(0, 2 * RET_WIDTH)
    cos = cos_ref[...]
    sin = sin_ref[...]
    for hh in range(RET_HEADS):
        sl = slice(hh * LANES, (hh + 1) * LANES)
        q = qk[:, hh * LANES:(hh + 1) * LANES]
        k = qk[:, RET_WIDTH + hh * LANES:RET_WIDTH + (hh + 1) * LANES]
        q = q * cos + pltpu.roll(q, LANES // 2, 1) * sin
        k = k * cos + pltpu.roll(k, LANES // 2, 1) * sin
        qd_ref[:, sl] = (q * dq_ref[:, sl]).astype(BF16)
        kd_ref[:, sl] = (k * dk_ref[:, sl]).astype(BF16)

    vg = proj(2 * RET_WIDTH, 4 * RET_WIDTH)
    v_ref[...] = vg[:, :RET_WIDTH].astype(BF16)
    g = vg[:, RET_WIDTH:]
    sg_ref[...] = _silu(g)

    ri = lax.broadcasted_iota(jnp.int32, (dense, dense), 0)
    ci = lax.broadcasted_iota(jnp.int32, (dense, dense), 1)
    mask = ci <= ri
    if seg < dense:
        shift = seg.bit_length() - 1
        mask = mask & ((ri >> shift) == (ci >> shift))
    n_seg = dense // seg
    n_blk = tm // dense

    def retention_block(blk):
        r0 = blk * dense
        for hh in range(RET_HEADS):
            sl = slice(hh * LANES, (hh + 1) * LANES)
            qd = qd_ref[r0:r0 + dense, sl]
            kd = kd_ref[r0:r0 + dense, sl]
            vv = v_ref[r0:r0 + dense, sl]
            sc = lax.dot_general(qd, kd, (((1,), (1,)), ((), ())), preferred_element_type=F32)
            sc = jnp.where(mask, sc, 0.0).astype(BF16)
            intra = jnp.dot(sc, vv, preferred_element_type=F32)
            gam_seg = (1.0 - 2.0 ** (-5.0 - hh)) ** seg
            for sg_i in range(n_seg):
                a0 = sg_i * seg
                row = (blk * n_seg + sg_i) if rows > 1 else 0
                st = st_ref[row, hh]
                cross = jnp.dot(qd[a0:a0 + seg], st.astype(BF16), preferred_element_type=F32)
                ret_ref[r0 + a0:r0 + a0 + seg, sl] = intra[a0:a0 + seg] + cross
                kv = lax.dot_general(kd[a0:a0 + seg], vv[a0:a0 + seg], (((0,), (0,)), ((), ())),
                                     preferred_element_type=F32)
                st_ref[row, hh] = gam_seg * (st + kv)

    for blk in range((n_blk + 1) // 2):
        retention_block(blk)

    uv = proj(c_uv, c_ar)
    u_ref[...] = uv[:, :GMLP_WIDTH]
    vm = uv[:, GMLP_WIDTH:]
    vc = vm - jnp.mean(vm, axis=-1, keepdims=True)
    vn = vc * lax.rsqrt(jnp.mean(vc * vc, axis=-1, keepdims=True) + EPS)
    vn_ref[...] = vn * lnw_ref[...] + lnb_ref[...]

    for blk in range((n_blk + 1) // 2, n_blk):
        retention_block(blk)

    mrg_ref[...] = _sigmoid(proj(c_ar, c_am))
    if casts is not None:
        casts()

    for hh in range(RET_HEADS):
        sl = slice(hh * LANES, (hh + 1) * LANES)
        o = ret_ref[:, sl]
        oc = o - jnp.mean(o, axis=-1, keepdims=True)
        on = oc * lax.rsqrt(jnp.mean(oc * oc, axis=-1, keepdims=True) + EPS)
        gat_ref[:, sl] = (on * gnw_ref[:, sl] * sg_ref[:, sl]).astype(BF16)

    n_g = tm // gseg
    wi = lax.broadcasted_iota(jnp.int32, (gseg, gseg), 0)
    wj = lax.broadcasted_iota(jnp.int32, (gseg, gseg), 1)
    for gg in range(GMLP_GROUPS):
        sl = slice(gg * LANES, (gg + 1) * LANES)
        w = jnp.where(wj <= wi, ws_ref[gg, 0:gseg, 0:gseg], 0.0).astype(BF16)
        rhs = jnp.concatenate([vn_ref[c * gseg:(c + 1) * gseg, sl] for c in range(n_g)], axis=1)
        sp = jnp.dot(w, rhs.astype(BF16), preferred_element_type=F32)
        bias = bst_ref[0:gseg, gg:gg + 1]
        for c in range(n_g):
            s_c = sp[:, c * LANES:(c + 1) * LANES] + bias
            gm_ref[c * gseg:(c + 1) * gseg, sl] = (u_ref[c * gseg:(c + 1) * gseg, sl] * s_c).astype(BF16)

    g_mlp = _sigmoid(proj(c_am, IN_COLS))
    y_ret = jnp.dot(gat_ref[...], wbr_ref[...], preferred_element_type=F32)
    y_mlp = jnp.dot(gm_ref[...], wbm_ref[...], preferred_element_type=F32)
    merged = (mrg_ref[...] * y_ret + g_mlp * y_mlp).astype(BF16)
    o_ref[...] = x_ref[...] + jnp.dot(merged, wout_ref[...], preferred_element_type=F32)


def _mix(x, tabs, gn_w, ln_w, ln_b, ws, bst, weights, s0, layer, *,
         tm, parts, dense, seg, gseg, rows, n_outer, n_inner, carry, emit_vn, casts=()):
    w_in, w_br, w_bm, w_out = weights
    cos, sin, dq, dk = tabs
    has_s0 = s0 is not None
    n_tok = x.shape[0]
    n_state = n_outer * rows
    tb = parts * tm
    x_spec = pl.BlockSpec((tb, D_MODEL), lambda b, t: (b * n_inner + t, 0))
    if cos.shape[0] == tb:
        tab_spec = pl.BlockSpec((tb, LANES), lambda b, t: (0, 0))
    else:
        tab_spec = pl.BlockSpec((tb, LANES), lambda b, t: (t, 0))
    st_spec = pl.BlockSpec((rows, RET_HEADS, RET_KEY_DIM, RET_VAL_DIM), lambda b, t: (b, 0, 0, 0))
    in_specs = [
        x_spec, tab_spec, tab_spec,
        _const_spec((tm, RET_WIDTH)), _const_spec((tm, RET_WIDTH)),
        _const_spec((1, RET_WIDTH), layer),
        _const_spec((1, GMLP_WIDTH), layer), _const_spec((1, GMLP_WIDTH), layer),
        _const_spec((GMLP_GROUPS, GMLP_CHUNK, GMLP_CHUNK), layer),
        _const_spec((GMLP_CHUNK, GMLP_GROUPS), layer),
        _const_spec((D_MODEL, IN_COLS)), _const_spec((RET_WIDTH, D_MODEL)),
        _const_spec((GMLP_WIDTH, D_MODEL)), _const_spec((D_MODEL, D_MODEL)),
    ]
    args = [x, cos, sin, dq, dk, gn_w, ln_w, ln_b, ws, bst, w_in, w_br, w_bm, w_out]
    if has_s0:
        in_specs.append(pl.BlockSpec((None, rows, RET_HEADS, RET_KEY_DIM, RET_VAL_DIM),
                                     lambda b, t: (layer, b, 0, 0, 0)))
        args.append(s0)
    out_shape = [jax.ShapeDtypeStruct((n_tok, D_MODEL), F32),
                 jax.ShapeDtypeStruct((n_state, RET_HEADS, RET_KEY_DIM, RET_VAL_DIM), F32)]
    out_specs = [x_spec, st_spec]
    scratch = [
        pltpu.VMEM((tm, D_MODEL), BF16),
        pltpu.VMEM((tm, RET_WIDTH), BF16),
        pltpu.VMEM((tm, RET_WIDTH), BF16),
        pltpu.VMEM((tm, RET_WIDTH), BF16),
        pltpu.VMEM((tm, RET_WIDTH), F32),
        pltpu.VMEM((tm, RET_WIDTH), F32),
        pltpu.VMEM((tm, D_MODEL), F32),
        pltpu.VMEM((tm, GMLP_WIDTH), F32),
        pltpu.VMEM((tm, RET_WIDTH), BF16),
        pltpu.VMEM((tm, GMLP_WIDTH), BF16),
    ]
    if emit_vn:
        out_shape.append(jax.ShapeDtypeStruct((n_tok, GMLP_WIDTH), F32))
        out_specs.append(pl.BlockSpec((tb, GMLP_WIDTH), lambda b, t: (b * n_inner + t, 0)))
    else:
        scratch.append(pltpu.VMEM((tm, GMLP_WIDTH), F32))
    c_in, c_args, c_out, c_shapes, has_gain = _cast_specs(
        casts, n_outer * n_inner, lambda b, t: b * n_inner + t)
    return pl.pallas_call(
        functools.partial(_mix_kernel, tm=tm, parts=parts, dense=dense, seg=seg, gseg=gseg, rows=rows,
                          carry=carry, has_s0=has_s0, emit_vn=emit_vn, has_gain=has_gain),
        out_shape=out_shape + c_shapes,
        grid=(n_outer, n_inner),
        in_specs=in_specs + c_in,
        out_specs=out_specs + c_out,
        scratch_shapes=scratch,
        compiler_params=pltpu.CompilerParams(
            dimension_semantics=("arbitrary", "arbitrary"), vmem_limit_bytes=VMEM_LIMIT_BYTES),
        name="mix_rows" if rows > 1 else "mix_seq",
    )(*args, *c_args)


def kernel(x_prompt, x_sample, state_ret, ffn1_norm, ffn1_w_gu, ffn1_w_down, mix_norm, w_in, ret_gn_w, gmlp_ln_w, gmlp_ln_b, gmlp_ws, gmlp_bs, w_br_ret, w_br_mlp, w_out, ffn2_norm, ffn2_w_gu, ffn2_w_down, final_norm):
    bp, lp, _ = x_prompt.shape
    bs, ls, _ = x_sample.shape

    g1 = ffn1_norm[:, :, None]
    g2 = ffn2_norm[:, :, None]
    gm = mix_norm[:, :, None]
    stages = [((g1[0] * ffn1_w_gu[0]).astype(BF16), ffn1_w_down[0].astype(BF16))]
    gnw = ret_gn_w.reshape(DEPTH, 1, RET_WIDTH)
    lnw = gmlp_ln_w.reshape(DEPTH, 1, GMLP_WIDTH)
    lnb = gmlp_ln_b.reshape(DEPTH, 1, GMLP_WIDTH)
    bst = jnp.swapaxes(gmlp_bs, 1, 2)
    fw = final_norm.reshape(1, D_MODEL)

    half = RET_KEY_DIM // 2
    freqs = ROPE_BASE ** (-jnp.arange(half, dtype=F32) / half)
    freq2 = jnp.concatenate([freqs, freqs]).reshape(1, LANES)
    log_g = jnp.log1p(-jnp.exp2(-5.0 - jnp.arange(RET_HEADS, dtype=F32)))
    logg_lanes = jnp.repeat(log_g, LANES).reshape(1, RET_WIDTH)

    tm_p = 512
    parts_p = 2
    dense_p = 256
    rows_s = 8
    tm_s = rows_s * ls
    tabs_p = _tables(freq2, logg_lanes, n_pos=lp, pos0=0, period=lp, tm=tm_p, seg=dense_p)
    tabs_s = _tables(freq2, logg_lanes, n_pos=tm_s, pos0=PAST_LEN, period=ls, tm=tm_s, seg=ls)

    xp = x_prompt.reshape(bp * lp, D_MODEL)
    xs = x_sample.reshape(bs * ls, D_MODEL)
    sp_list, ss_list, vs_list = [], [], []
    for l in range(DEPTH):
        mix_casts = [(w_in, l, gm), (w_br_ret, l, None), (w_br_mlp, l, None), (w_out, l, None)]
        xp, *mix_w = _ffn(xp, stages, fw, tm=_ffn_tile(stages), final=False, casts=mix_casts)
        xs, = _ffn(xs, stages, fw, tm=256, final=False)
        ffn_casts = [(ffn2_w_gu, l, g2), (ffn2_w_down, l, None)]
        if l + 1 < DEPTH:
            ffn_casts += [(ffn1_w_gu, l + 1, g1), (ffn1_w_down, l + 1, None)]
        xp, sp, *ffn_w = _mix(xp, tabs_p, gnw, lnw, lnb, gmlp_ws, bst, mix_w,
                              None, l, tm=tm_p, parts=parts_p, dense=dense_p, seg=dense_p,
                              gseg=min(GMLP_CHUNK, lp), rows=1, n_outer=bp, n_inner=lp // (parts_p * tm_p),
                              carry=True, emit_vn=False, casts=ffn_casts)
        xs, ss, vs = _mix(xs, tabs_s, gnw, lnw, lnb, gmlp_ws, bst, mix_w,
                          state_ret, l, tm=tm_s, parts=1, dense=tm_s, seg=ls, gseg=min(GMLP_CHUNK, ls), rows=rows_s,
                          n_outer=bs // rows_s, n_inner=1, carry=False, emit_vn=True)
        stages = [tuple(ffn_w[i:i + 2]) for i in range(0, len(ffn_w), 2)]
        sp_list.append(sp)
        ss_list.append(ss)
        vs_list.append(vs.reshape(bs, ls, GMLP_WIDTH))
    xp, = _ffn(xp, stages, fw, tm=_ffn_tile(stages), final=True)
    xs, = _ffn(xs, stages, fw, tm=256, final=True)
    y_prompt = xp.reshape(bp, lp, D_MODEL)
    y_sample = xs.reshape(bs, ls, D_MODEL)
    return (y_prompt, y_sample, jnp.stack(sp_list), jnp.stack(ss_list), jnp.stack(vs_list))
```

```python
import functools

import jax
import jax.numpy as jnp
from jax import lax
from jax.experimental import pallas as pl
from jax.experimental.pallas import tpu as pltpu

D_MODEL = 1024
DEPTH = 4
PAST_LEN = 4096
RET_HEADS = 4
RET_KEY_DIM = 128
RET_VAL_DIM = 128
RET_WIDTH = RET_HEADS * RET_KEY_DIM
GMLP_CHUNK = 128
GMLP_GROUPS = 4
GMLP_WIDTH = 512
D_FF = 2816
ROPE_BASE = 10000.0
EPS = 1e-6
IN_COLS = 4 * RET_WIDTH + 2 * GMLP_WIDTH + 2 * D_MODEL

LANES = 128
FF_CHUNK = 256
VMEM_LIMIT_BYTES = 56 * 1024 * 1024

RET_BLOCK = 256
PROMPT_TILE = 512
PROMPT_PARTS = 2
SAMPLE_ROWS = 8
SAMPLE_FFN_TILE = 256

F32 = jnp.float32
BF16 = jnp.bfloat16


def _const_spec(shape, layer=None):
    nd = len(shape)
    if layer is None:
        return pl.BlockSpec(shape, lambda *_: (0,) * nd, pipeline_mode=pl.Buffered(1))
    return pl.BlockSpec((None,) + shape, lambda *_: (layer,) + (0,) * nd,
                        pipeline_mode=pl.Buffered(1))


BF16_SUBLANES = 16


def _cast_specs(casts, n_steps, linear_step):
    in_specs, args, out_specs, out_shapes, has_gain = [], [], [], [], []
    for src, layer, gain in casts:
        _, n_rows, n_cols = src.shape
        rb = BF16_SUBLANES
        while n_rows % rb or n_rows // rb > n_steps:
            rb += BF16_SUBLANES
        n_blk = n_rows // rb
        hold = n_steps // n_blk

        def blk(*g, hold=hold, n_blk=n_blk):
            return jnp.minimum(linear_step(*g) // hold, n_blk - 1)

        in_specs.append(pl.BlockSpec((None, rb, n_cols), lambda *g, blk=blk, layer=layer: (layer, blk(*g), 0)))
        args.append(src)
        if gain is not None:
            in_specs.append(pl.BlockSpec((None, rb, 1), lambda *g, blk=blk, layer=layer: (layer, blk(*g), 0)))
            args.append(gain)
        out_specs.append(pl.BlockSpec((rb, n_cols), lambda *g, blk=blk: (blk(*g), 0)))
        out_shapes.append(jax.ShapeDtypeStruct((n_rows, n_cols), BF16))
        has_gain.append(gain is not None)
    return in_specs, args, out_specs, out_shapes, tuple(has_gain)


def _run_casts(in_refs, out_refs, has_gain):
    it = iter(in_refs)
    for dst_ref, gained in zip(out_refs, has_gain):
        w = next(it)[...]
        if gained:
            w = w * next(it)[...]
        dst_ref[...] = w.astype(BF16)


def _rms_scale(x):
    return lax.rsqrt(jnp.mean(x * x, axis=-1, keepdims=True) + EPS)


def _rms(x, w):
    return x * _rms_scale(x) * w


def _sigmoid(x):
    return 0.5 * jnp.tanh(0.5 * x) + 0.5


def _silu(x):
    h = 0.5 * x
    return h * jnp.tanh(h) + h


def _ffn_kernel(*refs, n_stage, final, has_gain):
    x_ref = refs[0]
    stage_refs = refs[1:1 + 2 * n_stage]
    fw_ref = refs[1 + 2 * n_stage]
    n_cast_in = len(has_gain) + sum(has_gain)
    cast_in = refs[2 + 2 * n_stage:2 + 2 * n_stage + n_cast_in]
    o_ref = refs[2 + 2 * n_stage + n_cast_in]
    cast_out = refs[3 + 2 * n_stage + n_cast_in:3 + 2 * n_stage + n_cast_in + len(has_gain)]
    h_ref, act_ref = refs[3 + 2 * n_stage + n_cast_in + len(has_gain):]
    src_ref = x_ref
    for s in range(n_stage):
        wgu_ref, wd_ref = stage_refs[2 * s:2 * s + 2]
        x = src_ref[...]
        h_ref[...] = x.astype(BF16)
        rinv = _rms_scale(x)
        for c in range(D_FF // FF_CHUNK):
            if s == 0 and c == 2:
                _run_casts(cast_in, cast_out, has_gain)
            cols = slice(c * FF_CHUNK, (c + 1) * FF_CHUNK)
            a = rinv * jnp.dot(h_ref[...], wgu_ref[:, cols], preferred_element_type=F32)
            b = rinv * jnp.dot(h_ref[...], wgu_ref[:, D_FF + c * FF_CHUNK:D_FF + (c + 1) * FF_CHUNK],
                               preferred_element_type=F32)
            act_ref[:, cols] = (_silu(a) * b).astype(BF16)
        y = jnp.dot(act_ref[...], wd_ref[...], preferred_element_type=F32)
        out = src_ref[...] + 0.5 * y
        if final and s == n_stage - 1:
            out = _rms(out, fw_ref[...])
        o_ref[...] = out
        src_ref = o_ref


def _ffn_tile(stages):
    return 1024 if len(stages) == 1 else 512


def _ffn(x, stages, final_w, *, tm, final, casts=()):
    n = x.shape[0]
    in_specs = [pl.BlockSpec((tm, D_MODEL), lambda i: (i, 0))]
    args = [x]
    for w_gu, w_down in stages:
        in_specs += [_const_spec((D_MODEL, 2 * D_FF)), _const_spec((D_FF, D_MODEL))]
        args += [w_gu, w_down]
    in_specs.append(_const_spec((1, D_MODEL)))
    args.append(final_w)
    c_in, c_args, c_out, c_shapes, has_gain = _cast_specs(casts, n // tm, lambda i: i)
    return pl.pallas_call(
        functools.partial(_ffn_kernel, n_stage=len(stages), final=final, has_gain=has_gain),
        out_shape=[jax.ShapeDtypeStruct((n, D_MODEL), F32)] + c_shapes,
        grid=(n // tm,),
        in_specs=in_specs + c_in,
        out_specs=[pl.BlockSpec((tm, D_MODEL), lambda i: (i, 0))] + c_out,
        scratch_shapes=[pltpu.VMEM((tm, D_MODEL), BF16), pltpu.VMEM((tm, D_FF), BF16)],
        compiler_params=pltpu.CompilerParams(
            dimension_semantics=("arbitrary",), vmem_limit_bytes=VMEM_LIMIT_BYTES),
        name="ffn%d%s" % (len(stages), "_final" if final else ""),
    )(*args, *c_args)


def _table_kernel(freq_ref, logg_ref, cos_ref, sin_ref, dq_ref, dk_ref, bc_ref, bs_ref, *, pos0, period, seg):
    n = cos_ref.shape[0]
    pid = pl.program_id(0)

    @pl.when(pid == 0)
    def _():
        jb = jnp.bitwise_and(lax.broadcasted_iota(jnp.int32, (n, LANES), 0), period - 1).astype(F32)
        ang = jb * freq_ref[...]
        bc_ref[...] = jnp.cos(ang)
        bs_ref[...] = jnp.sin(ang)
        m = dq_ref.shape[0]
        j = jnp.bitwise_and(lax.broadcasted_iota(jnp.int32, (m, RET_WIDTH), 0), seg - 1).astype(F32)
        e = (j + 1.0) * logg_ref[...]
        dq_ref[...] = jnp.exp(e)
        dk_ref[...] = jnp.exp(-e) * (RET_KEY_DIM ** -0.5)

    off = (pos0 + pid * n).astype(F32) * freq_ref[...]
    co = jnp.cos(off)
    so = jnp.sin(off)
    bc = bc_ref[...]
    bs = bs_ref[...]
    cos_ref[...] = bc * co - bs * so
    s = bs * co + bc * so
    lane = lax.broadcasted_iota(jnp.int32, (n, LANES), 1)
    sin_ref[...] = jnp.where(lane < LANES // 2, -s, s)


def _tables(freq2, logg_lanes, *, n_pos, pos0, period, tm, seg):
    assert period & (period - 1) == 0 and seg & (seg - 1) == 0
    tb = min(n_pos, 512)
    assert period >= n_pos or n_pos == tb
    return pl.pallas_call(
        functools.partial(_table_kernel, pos0=pos0, period=period, seg=seg),
        out_shape=(
            jax.ShapeDtypeStruct((n_pos, LANES), F32),
            jax.ShapeDtypeStruct((n_pos, LANES), F32),
            jax.ShapeDtypeStruct((tm, RET_WIDTH), F32),
            jax.ShapeDtypeStruct((tm, RET_WIDTH), F32),
        ),
        grid=(n_pos // tb,),
        in_specs=[pl.BlockSpec((1, LANES), lambda i: (0, 0)),
                  pl.BlockSpec((1, RET_WIDTH), lambda i: (0, 0))],
        out_specs=(
            pl.BlockSpec((tb, LANES), lambda i: (i, 0)),
            pl.BlockSpec((tb, LANES), lambda i: (i, 0)),
            pl.BlockSpec((tm, RET_WIDTH), lambda i: (0, 0)),
            pl.BlockSpec((tm, RET_WIDTH), lambda i: (0, 0)),
        ),
        scratch_shapes=[pltpu.VMEM((tb, LANES), F32), pltpu.VMEM((tb, LANES), F32)],
        compiler_params=pltpu.CompilerParams(dimension_semantics=("arbitrary",)),
        name="tables",
    )(freq2, logg_lanes)


def _mix_kernel(*refs, tm, parts, dense, seg, gseg, rows, carry, has_s0, emit_vn, has_gain):
    it = iter(refs)
    x_ref, cos_ref, sin_ref, dq_ref, dk_ref = (next(it) for _ in range(5))
    gnw_ref, lnw_ref, lnb_ref, ws_ref, bst_ref = (next(it) for _ in range(5))
    win_ref, wbr_ref, wbm_ref, wout_ref = (next(it) for _ in range(4))
    s0_ref = next(it) if has_s0 else None
    cast_in = [next(it) for _ in range(len(has_gain) + sum(has_gain))]
    o_ref, st_ref = next(it), next(it)
    vn_ref = next(it) if emit_vn else None
    cast_out = [next(it) for _ in has_gain]
    scratch = [next(it) for _ in range(10)]
    vn_scratch = None if emit_vn else next(it)

    if carry:
        @pl.when(pl.program_id(1) == 0)
        def _():
            st_ref[...] = s0_ref[...] if has_s0 else jnp.zeros(st_ref.shape, F32)
    else:
        st_ref[...] = s0_ref[...] if has_s0 else jnp.zeros(st_ref.shape, F32)

    for part in range(parts):
        rs = slice(part * tm, (part + 1) * tm)
        casts = functools.partial(_run_casts, cast_in, cast_out, has_gain) if part == 0 else None
        _mix_tile(x_ref.at[rs], cos_ref.at[rs], sin_ref.at[rs], dq_ref, dk_ref,
                  gnw_ref, lnw_ref, lnb_ref, ws_ref, bst_ref, win_ref, wbr_ref, wbm_ref, wout_ref,
                  o_ref.at[rs], st_ref, vn_ref.at[rs] if emit_vn else vn_scratch, scratch, casts,
                  tm=tm, dense=dense, seg=seg, gseg=gseg, rows=rows)


def _mix_tile(x_ref, cos_ref, sin_ref, dq_ref, dk_ref, gnw_ref, lnw_ref, lnb_ref, ws_ref, bst_ref,
              win_ref, wbr_ref, wbm_ref, wout_ref, o_ref, st_ref, vn_ref, scratch, casts,
              *, tm, dense, seg, gseg, rows):
    h_ref, qd_ref, kd_ref, v_ref, sg_ref, ret_ref, mrg_ref, u_ref, gat_ref, gm_ref = scratch

    x = x_ref[...]
    h_ref[...] = x.astype(BF16)
    rinv = _rms_scale(x)

    c_uv = 4 * RET_WIDTH
    c_ar = c_uv + 2 * GMLP_WIDTH
    c_am = c_ar + D_MODEL

    def proj(lo, hi):
        return rinv * jnp.dot(h_ref[...], win_ref[:, lo:hi], preferred_element_type=F32)

    qk = proj(0, 2 * RET_WIDTH)
    cos = cos_ref[...]
    sin = sin_ref[...]
    for hh in range(RET_HEADS):
        sl = slice(hh * LANES, (hh + 1) * LANES)
        q = qk[:, hh * LANES:(hh + 1) * LANES]
        k = qk[:, RET_WIDTH + hh * LANES:RET_WIDTH + (hh + 1) * LANES]
        q = q * cos + pltpu.roll(q, LANES // 2, 1) * sin
        k = k * cos + pltpu.roll(k, LANES // 2, 1) * sin
        qd_ref[:, sl] = (q * dq_ref[:, sl]).astype(BF16)
        kd_ref[:, sl] = (k * dk_ref[:, sl]).astype(BF16)

    vg = proj(2 * RET_WIDTH, 4 * RET_WIDTH)
    v_ref[...] = vg[:, :RET_WIDTH].astype(BF16)
    g = vg[:, RET_WIDTH:]
    sg_ref[...] = _silu(g)

    ri = lax.broadcasted_iota(jnp.int32, (dense, dense), 0)
    ci = lax.broadcasted_iota(jnp.int32, (dense, dense), 1)
    mask = ci <= ri
    if seg < dense:
        shift = seg.bit_length() - 1
        mask = mask & ((ri >> shift) == (ci >> shift))
    n_seg = dense // seg
    n_blk = tm // dense

    def retention_block(blk):
        r0 = blk * dense
        for hh in range(RET_HEADS):
            sl = slice(hh * LANES, (hh + 1) * LANES)
            qd = qd_ref[r0:r0 + dense, sl]
            kd = kd_ref[r0:r0 + dense, sl]
            vv = v_ref[r0:r0 + dense, sl]
            sc = lax.dot_general(qd, kd, (((1,), (1,)), ((), ())), preferred_element_type=F32)
            sc = jnp.where(mask, sc, 0.0).astype(BF16)
            intra = jnp.dot(sc, vv, preferred_element_type=F32)
            gam_seg = (1.0 - 2.0 ** (-5.0 - hh)) ** seg
            for sg_i in range(n_seg):
                a0 = sg_i * seg
                row = (blk * n_seg + sg_i) if rows > 1 else 0
                st = st_ref[row, hh]
                cross = jnp.dot(qd[a0:a0 + seg], st.astype(BF16), preferred_element_type=F32)
                ret_ref[r0 + a0:r0 + a0 + seg, sl] = intra[a0:a0 + seg] + cross
                kv = lax.dot_general(kd[a0:a0 + seg], vv[a0:a0 + seg], (((0,), (0,)), ((), ())),
                                     preferred_element_type=F32)
                st_ref[row, hh] = gam_seg * (st + kv)

    for blk in range((n_blk + 1) // 2):
        retention_block(blk)

    uv = proj(c_uv, c_ar)
    u_ref[...] = uv[:, :GMLP_WIDTH]
    vm = uv[:, GMLP_WIDTH:]
    vc = vm - jnp.mean(vm, axis=-1, keepdims=True)
    vn = vc * lax.rsqrt(jnp.mean(vc * vc, axis=-1, keepdims=True) + EPS)
    vn_ref[...] = vn * lnw_ref[...] + lnb_ref[...]

    mrg_ref[...] = _sigmoid(proj(c_ar, c_am))

    for blk in range((n_blk + 1) // 2, n_blk):
        retention_block(blk)

    if casts is not None:
        casts()

    for hh in range(RET_HEADS):
        sl = slice(hh * LANES, (hh + 1) * LANES)
        o = ret_ref[:, sl]
        oc = o - jnp.mean(o, axis=-1, keepdims=True)
        on = oc * lax.rsqrt(jnp.mean(oc * oc, axis=-1, keepdims=True) + EPS)
        gat_ref[:, sl] = (on * gnw_ref[:, sl] * sg_ref[:, sl]).astype(BF16)

    n_g = tm // gseg
    wi = lax.broadcasted_iota(jnp.int32, (gseg, gseg), 0)
    wj = lax.broadcasted_iota(jnp.int32, (gseg, gseg), 1)
    for gg in range(GMLP_GROUPS):
        sl = slice(gg * LANES, (gg + 1) * LANES)
        w = jnp.where(wj <= wi, ws_ref[gg, 0:gseg, 0:gseg], 0.0).astype(BF16)
        rhs = jnp.concatenate([vn_ref[c * gseg:(c + 1) * gseg, sl] for c in range(n_g)], axis=1)
        sp = jnp.dot(w, rhs.astype(BF16), preferred_element_type=F32)
        bias = bst_ref[0:gseg, gg:gg + 1]
        for c in range(n_g):
            s_c = sp[:, c * LANES:(c + 1) * LANES] + bias
            gm_ref[c * gseg:(c + 1) * gseg, sl] = (u_ref[c * gseg:(c + 1) * gseg, sl] * s_c).astype(BF16)

    g_mlp = _sigmoid(proj(c_am, IN_COLS))
    y_ret = jnp.dot(gat_ref[...], wbr_ref[...], preferred_element_type=F32)
    y_mlp = jnp.dot(gm_ref[...], wbm_ref[...], preferred_element_type=F32)
    merged = (mrg_ref[...] * y_ret + g_mlp * y_mlp).astype(BF16)
    o_ref[...] = x_ref[...] + jnp.dot(merged, wout_ref[...], preferred_element_type=F32)


def _mix(x, tabs, gn_w, ln_w, ln_b, ws, bst, weights, s0, layer, *,
         tm, parts, dense, seg, gseg, rows, n_outer, n_inner, carry, emit_vn, casts=()):
    w_in, w_br, w_bm, w_out = weights
    cos, sin, dq, dk = tabs
    has_s0 = s0 is not None
    n_tok = x.shape[0]
    n_state = n_outer * rows
    tb = parts * tm
    x_spec = pl.BlockSpec((tb, D_MODEL), lambda b, t: (b * n_inner + t, 0))
    if cos.shape[0] == tb:
        tab_spec = pl.BlockSpec((tb, LANES), lambda b, t: (0, 0))
    else:
        tab_spec = pl.BlockSpec((tb, LANES), lambda b, t: (t, 0))
    st_spec = pl.BlockSpec((rows, RET_HEADS, RET_KEY_DIM, RET_VAL_DIM), lambda b, t: (b, 0, 0, 0))
    in_specs = [
        x_spec, tab_spec, tab_spec,
        _const_spec((tm, RET_WIDTH)), _const_spec((tm, RET_WIDTH)),
        _const_spec((1, RET_WIDTH), layer),
        _const_spec((1, GMLP_WIDTH), layer), _const_spec((1, GMLP_WIDTH), layer),
        _const_spec((GMLP_GROUPS, GMLP_CHUNK, GMLP_CHUNK), layer),
        _const_spec((GMLP_CHUNK, GMLP_GROUPS), layer),
        _const_spec((D_MODEL, IN_COLS)), _const_spec((RET_WIDTH, D_MODEL)),
        _const_spec((GMLP_WIDTH, D_MODEL)), _const_spec((D_MODEL, D_MODEL)),
    ]
    args = [x, cos, sin, dq, dk, gn_w, ln_w, ln_b, ws, bst, w_in, w_br, w_bm, w_out]
    if has_s0:
        in_specs.append(pl.BlockSpec((None, rows, RET_HEADS, RET_KEY_DIM, RET_VAL_DIM),
                                     lambda b, t: (layer, b, 0, 0, 0)))
        args.append(s0)
    out_shape = [jax.ShapeDtypeStruct((n_tok, D_MODEL), F32),
                 jax.ShapeDtypeStruct((n_state, RET_HEADS, RET_KEY_DIM, RET_VAL_DIM), F32)]
    out_specs = [x_spec, st_spec]
    scratch = [
        pltpu.VMEM((tm, D_MODEL), BF16),
        pltpu.VMEM((tm, RET_WIDTH), BF16),
        pltpu.VMEM((tm, RET_WIDTH), BF16),
        pltpu.VMEM((tm, RET_WIDTH), BF16),
        pltpu.VMEM((tm, RET_WIDTH), F32),
        pltpu.VMEM((tm, RET_WIDTH), F32),
        pltpu.VMEM((tm, D_MODEL), F32),
        pltpu.VMEM((tm, GMLP_WIDTH), F32),
        pltpu.VMEM((tm, RET_WIDTH), BF16),
        pltpu.VMEM((tm, GMLP_WIDTH), BF16),
    ]
    if emit_vn:
        out_shape.append(jax.ShapeDtypeStruct((n_tok, GMLP_WIDTH), F32))
        out_specs.append(pl.BlockSpec((tb, GMLP_WIDTH), lambda b, t: (b * n_inner + t, 0)))
    else:
        scratch.append(pltpu.VMEM((tm, GMLP_WIDTH), F32))
    c_in, c_args, c_out, c_shapes, has_gain = _cast_specs(
        casts, n_outer * n_inner, lambda b, t: b * n_inner + t)
    return pl.pallas_call(
        functools.partial(_mix_kernel, tm=tm, parts=parts, dense=dense, seg=seg, gseg=gseg, rows=rows,
                          carry=carry, has_s0=has_s0, emit_vn=emit_vn, has_gain=has_gain),
        out_shape=out_shape + c_shapes,
        grid=(n_outer, n_inner),
        in_specs=in_specs + c_in,
        out_specs=out_specs + c_out,
        scratch_shapes=scratch,
        compiler_params=pltpu.CompilerParams(
            dimension_semantics=("arbitrary", "arbitrary"), vmem_limit_bytes=VMEM_LIMIT_BYTES),
        name="mix_rows" if rows > 1 else "mix_seq",
    )(*args, *c_args)


def kernel(x_prompt, x_sample, state_ret, ffn1_norm, ffn1_w_gu, ffn1_w_down, mix_norm, w_in, ret_gn_w, gmlp_ln_w, gmlp_ln_b, gmlp_ws, gmlp_bs, w_br_ret, w_br_mlp, w_out, ffn2_norm, ffn2_w_gu, ffn2_w_down, final_norm):
    bp, lp, _ = x_prompt.shape
    bs, ls, _ = x_sample.shape

    g1 = ffn1_norm[:, :, None]
    g2 = ffn2_norm[:, :, None]
    gm = mix_norm[:, :, None]
    stages = [((g1[0] * ffn1_w_gu[0]).astype(BF16), ffn1_w_down[0].astype(BF16))]
    gnw = ret_gn_w.reshape(DEPTH, 1, RET_WIDTH)
    lnw = gmlp_ln_w.reshape(DEPTH, 1, GMLP_WIDTH)
    lnb = gmlp_ln_b.reshape(DEPTH, 1, GMLP_WIDTH)
    bst = jnp.swapaxes(gmlp_bs, 1, 2)
    fw = final_norm.reshape(1, D_MODEL)

    half = RET_KEY_DIM // 2
    freqs = ROPE_BASE ** (-jnp.arange(half, dtype=F32) / half)
    freq2 = jnp.concatenate([freqs, freqs]).reshape(1, LANES)
    log_g = jnp.log1p(-jnp.exp2(-5.0 - jnp.arange(RET_HEADS, dtype=F32)))
    logg_lanes = jnp.repeat(log_g, LANES).reshape(1, RET_WIDTH)

    tm_p, parts_p, dense_p = PROMPT_TILE, PROMPT_PARTS, RET_BLOCK
    rows_s = SAMPLE_ROWS
    tm_s = rows_s * ls
    assert lp % (parts_p * tm_p) == 0 and tm_p % dense_p == 0 and bs % rows_s == 0
    tabs_p = _tables(freq2, logg_lanes, n_pos=lp, pos0=0, period=lp, tm=tm_p, seg=dense_p)
    tabs_s = _tables(freq2, logg_lanes, n_pos=tm_s, pos0=PAST_LEN, period=ls, tm=tm_s, seg=ls)

    def mixer_casts(l):
        return [(w_in, l, gm), (w_br_ret, l, None), (w_br_mlp, l, None), (w_out, l, None)]

    xp = x_prompt.reshape(bp * lp, D_MODEL)
    xs = x_sample.reshape(bs * ls, D_MODEL)
    sp_list, ss_list, vs_list = [], [], []
    for l in range(DEPTH):
        xp, *cast_w = _ffn(xp, stages, fw, tm=_ffn_tile(stages), final=False,
                           casts=mixer_casts(0) if l == 0 else ())
        if l == 0:
            mix_w = cast_w
        xs, = _ffn(xs, stages, fw, tm=SAMPLE_FFN_TILE, final=False)
        next_casts = [(ffn2_w_gu, l, g2), (ffn2_w_down, l, None)]
        if l + 1 < DEPTH:
            next_casts += [(ffn1_w_gu, l + 1, g1), (ffn1_w_down, l + 1, None)] + mixer_casts(l + 1)
        xp, sp, *cast_w = _mix(xp, tabs_p, gnw, lnw, lnb, gmlp_ws, bst, mix_w,
                               None, l, tm=tm_p, parts=parts_p, dense=dense_p, seg=dense_p,
                               gseg=min(GMLP_CHUNK, lp), rows=1, n_outer=bp, n_inner=lp // (parts_p * tm_p),
                               carry=True, emit_vn=False, casts=next_casts)
        xs, ss, vs = _mix(xs, tabs_s, gnw, lnw, lnb, gmlp_ws, bst, mix_w,
                          state_ret, l, tm=tm_s, parts=1, dense=tm_s, seg=ls, gseg=min(GMLP_CHUNK, ls), rows=rows_s,
                          n_outer=bs // rows_s, n_inner=1, carry=False, emit_vn=True)
        n_ffn = 4 if l + 1 < DEPTH else 2
        stages = [tuple(cast_w[i:i + 2]) for i in range(0, n_ffn, 2)]
        mix_w = cast_w[n_ffn:]
        sp_list.append(sp)
        ss_list.append(ss)
        vs_list.append(vs.reshape(bs, ls, GMLP_WIDTH))
    xp, = _ffn(xp, stages, fw, tm=_ffn_tile(stages), final=True)
    xs, = _ffn(xs, stages, fw, tm=SAMPLE_FFN_TILE, final=True)
    y_prompt = xp.reshape(bp, lp, D_MODEL)
    y_sample = xs.reshape(bs, ls, D_MODEL)
    return (y_prompt, y_sample, jnp.stack(sp_list), jnp.stack(ss_list), jnp.stack(vs_list))
```

```python
import functools

import jax
import jax.numpy as jnp
from jax import lax
from jax.experimental import pallas as pl
from jax.experimental.pallas import tpu as pltpu

D_MODEL = 1024
DEPTH = 4
PAST_LEN = 4096
RET_HEADS = 4
RET_KEY_DIM = 128
RET_VAL_DIM = 128
RET_WIDTH = RET_HEADS * RET_KEY_DIM
GMLP_CHUNK = 128
GMLP_GROUPS = 4
GMLP_WIDTH = 512
D_FF = 2816
ROPE_BASE = 10000.0
EPS = 1e-6
IN_COLS = 4 * RET_WIDTH + 2 * GMLP_WIDTH + 2 * D_MODEL

LANES = 128
FF_CHUNK = 256
VMEM_LIMIT_BYTES = 56 * 1024 * 1024

RET_BLOCK = 256
PROMPT_TILE = 512
PROMPT_PARTS = 2
SAMPLE_ROWS = 16
SAMPLE_FFN_TILE = 512

F32 = jnp.float32
BF16 = jnp.bfloat16


def _const_spec(shape, layer=None):
    nd = len(shape)
    if layer is None:
        return pl.BlockSpec(shape, lambda *_: (0,) * nd, pipeline_mode=pl.Buffered(1))
    return pl.BlockSpec((None,) + shape, lambda *_: (layer,) + (0,) * nd,
                        pipeline_mode=pl.Buffered(1))


BF16_SUBLANES = 16


def _cast_specs(casts, n_steps, linear_step):
    in_specs, args, out_specs, out_shapes, has_gain = [], [], [], [], []
    for src, layer, gain in casts:
        _, n_rows, n_cols = src.shape
        rb = BF16_SUBLANES
        while n_rows % rb or n_rows // rb > n_steps:
            rb += BF16_SUBLANES
        n_blk = n_rows // rb
        hold = n_steps // n_blk

        def blk(*g, hold=hold, n_blk=n_blk):
            return jnp.minimum(linear_step(*g) // hold, n_blk - 1)

        in_specs.append(pl.BlockSpec((None, rb, n_cols), lambda *g, blk=blk, layer=layer: (layer, blk(*g), 0)))
        args.append(src)
        if gain is not None:
            in_specs.append(pl.BlockSpec((None, rb, 1), lambda *g, blk=blk, layer=layer: (layer, blk(*g), 0)))
            args.append(gain)
        out_specs.append(pl.BlockSpec((rb, n_cols), lambda *g, blk=blk: (blk(*g), 0)))
        out_shapes.append(jax.ShapeDtypeStruct((n_rows, n_cols), BF16))
        has_gain.append(gain is not None)
    return in_specs, args, out_specs, out_shapes, tuple(has_gain)


def _run_casts(in_refs, out_refs, has_gain):
    it = iter(in_refs)
    for dst_ref, gained in zip(out_refs, has_gain):
        w = next(it)[...]
        if gained:
            w = w * next(it)[...]
        dst_ref[...] = w.astype(BF16)


def _rms_scale(x):
    return lax.rsqrt(jnp.mean(x * x, axis=-1, keepdims=True) + EPS)


def _rms(x, w):
    return x * _rms_scale(x) * w


def _sigmoid(x):
    return 0.5 * jnp.tanh(0.5 * x) + 0.5


def _silu(x):
    h = 0.5 * x
    return h * jnp.tanh(h) + h


def _ffn_kernel(*refs, n_stage, final, has_gain):
    x_ref = refs[0]
    stage_refs = refs[1:1 + 2 * n_stage]
    fw_ref = refs[1 + 2 * n_stage]
    n_cast_in = len(has_gain) + sum(has_gain)
    cast_in = refs[2 + 2 * n_stage:2 + 2 * n_stage + n_cast_in]
    o_ref = refs[2 + 2 * n_stage + n_cast_in]
    cast_out = refs[3 + 2 * n_stage + n_cast_in:3 + 2 * n_stage + n_cast_in + len(has_gain)]
    h_ref, act_ref = refs[3 + 2 * n_stage + n_cast_in + len(has_gain):]
    src_ref = x_ref
    for s in range(n_stage):
        wgu_ref, wd_ref = stage_refs[2 * s:2 * s + 2]
        x = src_ref[...]
        h_ref[...] = x.astype(BF16)
        rinv = _rms_scale(x)
        for c in range(D_FF // FF_CHUNK):
            if s == 0 and c == 2:
                _run_casts(cast_in, cast_out, has_gain)
            cols = slice(c * FF_CHUNK, (c + 1) * FF_CHUNK)
            a = rinv * jnp.dot(h_ref[...], wgu_ref[:, cols], preferred_element_type=F32)
            b = rinv * jnp.dot(h_ref[...], wgu_ref[:, D_FF + c * FF_CHUNK:D_FF + (c + 1) * FF_CHUNK],
                               preferred_element_type=F32)
            act_ref[:, cols] = (_silu(a) * b).astype(BF16)
        y = jnp.dot(act_ref[...], wd_ref[...], preferred_element_type=F32)
        out = src_ref[...] + 0.5 * y
        if final and s == n_stage - 1:
            out = _rms(out, fw_ref[...])
        o_ref[...] = out
        src_ref = o_ref


def _ffn_tile(stages):
    return 1024 if len(stages) == 1 else 512


def _ffn(x, stages, final_w, *, tm, final, casts=()):
    n = x.shape[0]
    in_specs = [pl.BlockSpec((tm, D_MODEL), lambda i: (i, 0))]
    args = [x]
    for w_gu, w_down in stages:
        in_specs += [_const_spec((D_MODEL, 2 * D_FF)), _const_spec((D_FF, D_MODEL))]
        args += [w_gu, w_down]
    in_specs.append(_const_spec((1, D_MODEL)))
    args.append(final_w)
    c_in, c_args, c_out, c_shapes, has_gain = _cast_specs(casts, n // tm, lambda i: i)
    return pl.pallas_call(
        functools.partial(_ffn_kernel, n_stage=len(stages), final=final, has_gain=has_gain),
        out_shape=[jax.ShapeDtypeStruct((n, D_MODEL), F32)] + c_shapes,
        grid=(n // tm,),
        in_specs=in_specs + c_in,
        out_specs=[pl.BlockSpec((tm, D_MODEL), lambda i: (i, 0))] + c_out,
        scratch_shapes=[pltpu.VMEM((tm, D_MODEL), BF16), pltpu.VMEM((tm, D_FF), BF16)],
        compiler_params=pltpu.CompilerParams(
            dimension_semantics=("arbitrary",), vmem_limit_bytes=VMEM_LIMIT_BYTES),
        name="ffn%d%s" % (len(stages), "_final" if final else ""),
    )(*args, *c_args)


def _table_kernel(freq_ref, logg_ref, cos_ref, sin_ref, dq_ref, dk_ref, bc_ref, bs_ref, *, pos0, period, seg):
    n = cos_ref.shape[0]
    pid = pl.program_id(0)

    @pl.when(pid == 0)
    def _():
        jb = jnp.bitwise_and(lax.broadcasted_iota(jnp.int32, (n, LANES), 0), period - 1).astype(F32)
        ang = jb * freq_ref[...]
        bc_ref[...] = jnp.cos(ang)
        bs_ref[...] = jnp.sin(ang)
        m = dq_ref.shape[0]
        j = jnp.bitwise_and(lax.broadcasted_iota(jnp.int32, (m, RET_WIDTH), 0), seg - 1).astype(F32)
        e = (j + 1.0) * logg_ref[...]
        dq_ref[...] = jnp.exp(e)
        dk_ref[...] = jnp.exp(-e) * (RET_KEY_DIM ** -0.5)

    off = (pos0 + pid * n).astype(F32) * freq_ref[...]
    co = jnp.cos(off)
    so = jnp.sin(off)
    bc = bc_ref[...]
    bs = bs_ref[...]
    cos_ref[...] = bc * co - bs * so
    s = bs * co + bc * so
    lane = lax.broadcasted_iota(jnp.int32, (n, LANES), 1)
    sin_ref[...] = jnp.where(lane < LANES // 2, -s, s)


def _tables(freq2, logg_lanes, *, n_pos, pos0, period, tm, seg):
    assert period & (period - 1) == 0 and seg & (seg - 1) == 0
    tb = min(n_pos, 512)
    assert period >= n_pos or n_pos == tb
    return pl.pallas_call(
        functools.partial(_table_kernel, pos0=pos0, period=period, seg=seg),
        out_shape=(
            jax.ShapeDtypeStruct((n_pos, LANES), F32),
            jax.ShapeDtypeStruct((n_pos, LANES), F32),
            jax.ShapeDtypeStruct((tm, RET_WIDTH), F32),
            jax.ShapeDtypeStruct((tm, RET_WIDTH), F32),
        ),
        grid=(n_pos // tb,),
        in_specs=[pl.BlockSpec((1, LANES), lambda i: (0, 0)),
                  pl.BlockSpec((1, RET_WIDTH), lambda i: (0, 0))],
        out_specs=(
            pl.BlockSpec((tb, LANES), lambda i: (i, 0)),
            pl.BlockSpec((tb, LANES), lambda i: (i, 0)),
            pl.BlockSpec((tm, RET_WIDTH), lambda i: (0, 0)),
            pl.BlockSpec((tm, RET_WIDTH), lambda i: (0, 0)),
        ),
        scratch_shapes=[pltpu.VMEM((tb, LANES), F32), pltpu.VMEM((tb, LANES), F32)],
        compiler_params=pltpu.CompilerParams(dimension_semantics=("arbitrary",)),
        name="tables",
    )(freq2, logg_lanes)


def _mix_kernel(*refs, tm, parts, dense, seg, gseg, rows, carry, has_s0, emit_vn, has_gain):
    it = iter(refs)
    x_ref, cos_ref, sin_ref, dq_ref, dk_ref = (next(it) for _ in range(5))
    gnw_ref, lnw_ref, lnb_ref, ws_ref, bst_ref = (next(it) for _ in range(5))
    win_ref, wbr_ref, wbm_ref, wout_ref = (next(it) for _ in range(4))
    s0_ref = next(it) if has_s0 else None
    cast_in = [next(it) for _ in range(len(has_gain) + sum(has_gain))]
    o_ref, st_ref = next(it), next(it)
    vn_ref = next(it) if emit_vn else None
    cast_out = [next(it) for _ in has_gain]
    scratch = [next(it) for _ in range(10)]
    vn_scratch = None if emit_vn else next(it)

    if carry:
        @pl.when(pl.program_id(1) == 0)
        def _():
            st_ref[...] = s0_ref[...] if has_s0 else jnp.zeros(st_ref.shape, F32)
    else:
        st_ref[...] = s0_ref[...] if has_s0 else jnp.zeros(st_ref.shape, F32)

    for part in range(parts):
        rs = slice(part * tm, (part + 1) * tm)
        casts = functools.partial(_run_casts, cast_in, cast_out, has_gain) if part == 0 else None
        _mix_tile(x_ref.at[rs], cos_ref.at[rs], sin_ref.at[rs], dq_ref, dk_ref,
                  gnw_ref, lnw_ref, lnb_ref, ws_ref, bst_ref, win_ref, wbr_ref, wbm_ref, wout_ref,
                  o_ref.at[rs], st_ref, vn_ref.at[rs] if emit_vn else vn_scratch, scratch, casts,
                  tm=tm, dense=dense, seg=seg, gseg=gseg, rows=rows)


def _mix_tile(x_ref, cos_ref, sin_ref, dq_ref, dk_ref, gnw_ref, lnw_ref, lnb_ref, ws_ref, bst_ref,
              win_ref, wbr_ref, wbm_ref, wout_ref, o_ref, st_ref, vn_ref, scratch, casts,
              *, tm, dense, seg, gseg, rows):
    h_ref, qd_ref, kd_ref, v_ref, sg_ref, ret_ref, mrg_ref, u_ref, gat_ref, gm_ref = scratch

    x = x_ref[...]
    h_ref[...] = x.astype(BF16)
    rinv = _rms_scale(x)

    c_uv = 4 * RET_WIDTH
    c_ar = c_uv + 2 * GMLP_WIDTH
    c_am = c_ar + D_MODEL

    def proj(lo, hi):
        return rinv * jnp.dot(h_ref[...], win_ref[:, lo:hi], preferred_element_type=F32)

    qk = proj(0, 2 * RET_WIDTH)
    cos = cos_ref[...]
    sin = sin_ref[...]
    for hh in range(RET_HEADS):
        sl = slice(hh * LANES, (hh + 1) * LANES)
        q = qk[:, hh * LANES:(hh + 1) * LANES]
        k = qk[:, RET_WIDTH + hh * LANES:RET_WIDTH + (hh + 1) * LANES]
        q = q * cos + pltpu.roll(q, LANES // 2, 1) * sin
        k = k * cos + pltpu.roll(k, LANES // 2, 1) * sin
        qd_ref[:, sl] = (q * dq_ref[:, sl]).astype(BF16)
        kd_ref[:, sl] = (k * dk_ref[:, sl]).astype(BF16)

    vg = proj(2 * RET_WIDTH, 4 * RET_WIDTH)
    v_ref[...] = vg[:, :RET_WIDTH].astype(BF16)
    g = vg[:, RET_WIDTH:]
    sg_ref[...] = _silu(g)

    ri = lax.broadcasted_iota(jnp.int32, (dense, dense), 0)
    ci = lax.broadcasted_iota(jnp.int32, (dense, dense), 1)
    mask = ci <= ri
    if seg < dense:
        shift = seg.bit_length() - 1
        mask = mask & ((ri >> shift) == (ci >> shift))
    n_seg = dense // seg
    n_blk = tm // dense

    def retention_block(blk):
        r0 = blk * dense
        for hh in range(RET_HEADS):
            sl = slice(hh * LANES, (hh + 1) * LANES)
            qd = qd_ref[r0:r0 + dense, sl]
            kd = kd_ref[r0:r0 + dense, sl]
            vv = v_ref[r0:r0 + dense, sl]
            sc = lax.dot_general(qd, kd, (((1,), (1,)), ((), ())), preferred_element_type=F32)
            sc = jnp.where(mask, sc, 0.0).astype(BF16)
            intra = jnp.dot(sc, vv, preferred_element_type=F32)
            gam_seg = (1.0 - 2.0 ** (-5.0 - hh)) ** seg
            for sg_i in range(n_seg):
                a0 = sg_i * seg
                row = (blk * n_seg + sg_i) if rows > 1 else 0
                st = st_ref[row, hh]
                cross = jnp.dot(qd[a0:a0 + seg], st.astype(BF16), preferred_element_type=F32)
                ret_ref[r0 + a0:r0 + a0 + seg, sl] = intra[a0:a0 + seg] + cross
                kv = lax.dot_general(kd[a0:a0 + seg], vv[a0:a0 + seg], (((0,), (0,)), ((), ())),
                                     preferred_element_type=F32)
                st_ref[row, hh] = gam_seg * (st + kv)

    for blk in range((n_blk + 1) // 2):
        retention_block(blk)

    uv = proj(c_uv, c_ar)
    u_ref[...] = uv[:, :GMLP_WIDTH]
    vm = uv[:, GMLP_WIDTH:]
    vc = vm - jnp.mean(vm, axis=-1, keepdims=True)
    vn = vc * lax.rsqrt(jnp.mean(vc * vc, axis=-1, keepdims=True) + EPS)
    vn_ref[...] = vn * lnw_ref[...] + lnb_ref[...]

    mrg_ref[...] = _sigmoid(proj(c_ar, c_am))

    for blk in range((n_blk + 1) // 2, n_blk):
        retention_block(blk)

    if casts is not None:
        casts()

    for hh in range(RET_HEADS):
        sl = slice(hh * LANES, (hh + 1) * LANES)
        o = ret_ref[:, sl]
        oc = o - jnp.mean(o, axis=-1, keepdims=True)
        on = oc * lax.rsqrt(jnp.mean(oc * oc, axis=-1, keepdims=True) + EPS)
        gat_ref[:, sl] = (on * gnw_ref[:, sl] * sg_ref[:, sl]).astype(BF16)

    n_g = tm // gseg
    wi = lax.broadcasted_iota(jnp.int32, (gseg, gseg), 0)
    wj = lax.broadcasted_iota(jnp.int32, (gseg, gseg), 1)
    for gg in range(GMLP_GROUPS):
        sl = slice(gg * LANES, (gg + 1) * LANES)
        w = jnp.where(wj <= wi, ws_ref[gg, 0:gseg, 0:gseg], 0.0).astype(BF16)
        rhs = jnp.concatenate([vn_ref[c * gseg:(c + 1) * gseg, sl] for c in range(n_g)], axis=1)
        sp = jnp.dot(w, rhs.astype(BF16), preferred_element_type=F32)
        bias = bst_ref[0:gseg, gg:gg + 1]
        for c in range(n_g):
            s_c = sp[:, c * LANES:(c + 1) * LANES] + bias
            gm_ref[c * gseg:(c + 1) * gseg, sl] = (u_ref[c * gseg:(c + 1) * gseg, sl] * s_c).astype(BF16)

    g_mlp = _sigmoid(proj(c_am, IN_COLS))
    y_ret = jnp.dot(gat_ref[...], wbr_ref[...], preferred_element_type=F32)
    y_mlp = jnp.dot(gm_ref[...], wbm_ref[...], preferred_element_type=F32)
    merged = (mrg_ref[...] * y_ret + g_mlp * y_mlp).astype(BF16)
    o_ref[...] = x_ref[...] + jnp.dot(merged, wout_ref[...], preferred_element_type=F32)


def _mix(x, tabs, gn_w, ln_w, ln_b, ws, bst, weights, s0, layer, *,
         tm, parts, dense, seg, gseg, rows, n_outer, n_inner, carry, emit_vn, casts=()):
    w_in, w_br, w_bm, w_out = weights
    cos, sin, dq, dk = tabs
    has_s0 = s0 is not None
    n_tok = x.shape[0]
    n_state = n_outer * rows
    tb = parts * tm
    x_spec = pl.BlockSpec((tb, D_MODEL), lambda b, t: (b * n_inner + t, 0))
    if cos.shape[0] == tb:
        tab_spec = pl.BlockSpec((tb, LANES), lambda b, t: (0, 0))
    else:
        tab_spec = pl.BlockSpec((tb, LANES), lambda b, t: (t, 0))
    st_spec = pl.BlockSpec((rows, RET_HEADS, RET_KEY_DIM, RET_VAL_DIM), lambda b, t: (b, 0, 0, 0))
    in_specs = [
        x_spec, tab_spec, tab_spec,
        _const_spec((tm, RET_WIDTH)), _const_spec((tm, RET_WIDTH)),
        _const_spec((1, RET_WIDTH), layer),
        _const_spec((1, GMLP_WIDTH), layer), _const_spec((1, GMLP_WIDTH), layer),
        _const_spec((GMLP_GROUPS, GMLP_CHUNK, GMLP_CHUNK), layer),
        _const_spec((GMLP_CHUNK, GMLP_GROUPS), layer),
        _const_spec((D_MODEL, IN_COLS)), _const_spec((RET_WIDTH, D_MODEL)),
        _const_spec((GMLP_WIDTH, D_MODEL)), _const_spec((D_MODEL, D_MODEL)),
    ]
    args = [x, cos, sin, dq, dk, gn_w, ln_w, ln_b, ws, bst, w_in, w_br, w_bm, w_out]
    if has_s0:
        in_specs.append(pl.BlockSpec((None, rows, RET_HEADS, RET_KEY_DIM, RET_VAL_DIM),
                                     lambda b, t: (layer, b, 0, 0, 0)))
        args.append(s0)
    out_shape = [jax.ShapeDtypeStruct((n_tok, D_MODEL), F32),
                 jax.ShapeDtypeStruct((n_state, RET_HEADS, RET_KEY_DIM, RET_VAL_DIM), F32)]
    out_specs = [x_spec, st_spec]
    scratch = [
        pltpu.VMEM((tm, D_MODEL), BF16),
        pltpu.VMEM((tm, RET_WIDTH), BF16),
        pltpu.VMEM((tm, RET_WIDTH), BF16),
        pltpu.VMEM((tm, RET_WIDTH), BF16),
        pltpu.VMEM((tm, RET_WIDTH), F32),
        pltpu.VMEM((tm, RET_WIDTH), F32),
        pltpu.VMEM((tm, D_MODEL), F32),
        pltpu.VMEM((tm, GMLP_WIDTH), F32),
        pltpu.VMEM((tm, RET_WIDTH), BF16),
        pltpu.VMEM((tm, GMLP_WIDTH), BF16),
    ]
    if emit_vn:
        out_shape.append(jax.ShapeDtypeStruct((n_tok, GMLP_WIDTH), F32))
        out_specs.append(pl.BlockSpec((tb, GMLP_WIDTH), lambda b, t: (b * n_inner + t, 0)))
    else:
        scratch.append(pltpu.VMEM((tm, GMLP_WIDTH), F32))
    c_in, c_args, c_out, c_shapes, has_gain = _cast_specs(
        casts, n_outer * n_inner, lambda b, t: b * n_inner + t)
    return pl.pallas_call(
        functools.partial(_mix_kernel, tm=tm, parts=parts, dense=dense, seg=seg, gseg=gseg, rows=rows,
                          carry=carry, has_s0=has_s0, emit_vn=emit_vn, has_gain=has_gain),
        out_shape=out_shape + c_shapes,
        grid=(n_outer, n_inner),
        in_specs=in_specs + c_in,
        out_specs=out_specs + c_out,
        scratch_shapes=scratch,
        compiler_params=pltpu.CompilerParams(
            dimension_semantics=("arbitrary", "arbitrary"), vmem_limit_bytes=VMEM_LIMIT_BYTES),
        name="mix_rows" if rows > 1 else "mix_seq",
    )(*args, *c_args)


def kernel(x_prompt, x_sample, state_ret, ffn1_norm, ffn1_w_gu, ffn1_w_down, mix_norm, w_in, ret_gn_w, gmlp_ln_w, gmlp_ln_b, gmlp_ws, gmlp_bs, w_br_ret, w_br_mlp, w_out, ffn2_norm, ffn2_w_gu, ffn2_w_down, final_norm):
    bp, lp, _ = x_prompt.shape
    bs, ls, _ = x_sample.shape

    g1 = ffn1_norm[:, :, None]
    g2 = ffn2_norm[:, :, None]
    gm = mix_norm[:, :, None]
    stages = [((g1[0] * ffn1_w_gu[0]).astype(BF16), ffn1_w_down[0].astype(BF16))]
    gnw = ret_gn_w.reshape(DEPTH, 1, RET_WIDTH)
    lnw = gmlp_ln_w.reshape(DEPTH, 1, GMLP_WIDTH)
    lnb = gmlp_ln_b.reshape(DEPTH, 1, GMLP_WIDTH)
    bst = jnp.swapaxes(gmlp_bs, 1, 2)
    fw = final_norm.reshape(1, D_MODEL)

    half = RET_KEY_DIM // 2
    freqs = ROPE_BASE ** (-jnp.arange(half, dtype=F32) / half)
    freq2 = jnp.concatenate([freqs, freqs]).reshape(1, LANES)
    log_g = jnp.log1p(-jnp.exp2(-5.0 - jnp.arange(RET_HEADS, dtype=F32)))
    logg_lanes = jnp.repeat(log_g, LANES).reshape(1, RET_WIDTH)

    tm_p, parts_p, dense_p = PROMPT_TILE, PROMPT_PARTS, RET_BLOCK
    rows_s = SAMPLE_ROWS
    tm_s = rows_s * ls
    assert lp % (parts_p * tm_p) == 0 and tm_p % dense_p == 0 and bs % rows_s == 0
    tabs_p = _tables(freq2, logg_lanes, n_pos=lp, pos0=0, period=lp, tm=tm_p, seg=dense_p)
    tabs_s = _tables(freq2, logg_lanes, n_pos=tm_s, pos0=PAST_LEN, period=ls, tm=tm_s, seg=ls)

    def mixer_casts(l):
        return [(w_in, l, gm), (w_br_ret, l, None), (w_br_mlp, l, None), (w_out, l, None)]

    xp = x_prompt.reshape(bp * lp, D_MODEL)
    xs = x_sample.reshape(bs * ls, D_MODEL)
    sp_list, ss_list, vs_list = [], [], []
    for l in range(DEPTH):
        xp, *cast_w = _ffn(xp, stages, fw, tm=_ffn_tile(stages), final=False,
                           casts=mixer_casts(0) if l == 0 else ())
        if l == 0:
            mix_w = cast_w
        xs, = _ffn(xs, stages, fw, tm=SAMPLE_FFN_TILE, final=False)
        next_casts = [(ffn2_w_gu, l, g2), (ffn2_w_down, l, None)]
        if l + 1 < DEPTH:
            next_casts += [(ffn1_w_gu, l + 1, g1), (ffn1_w_down, l + 1, None)] + mixer_casts(l + 1)
        xp, sp, *cast_w = _mix(xp, tabs_p, gnw, lnw, lnb, gmlp_ws, bst, mix_w,
                               None, l, tm=tm_p, parts=parts_p, dense=dense_p, seg=dense_p,
                               gseg=min(GMLP_CHUNK, lp), rows=1, n_outer=bp, n_inner=lp // (parts_p * tm_p),
                               carry=True, emit_vn=False, casts=next_casts)
        xs, ss, vs = _mix(xs, tabs_s, gnw, lnw, lnb, gmlp_ws, bst, mix_w,
                          state_ret, l, tm=tm_s, parts=1, dense=tm_s, seg=ls, gseg=min(GMLP_CHUNK, ls), rows=rows_s,
                          n_outer=bs // rows_s, n_inner=1, carry=False, emit_vn=True)
        n_ffn = 4 if l + 1 < DEPTH else 2
        stages = [tuple(cast_w[i:i + 2]) for i in range(0, n_ffn, 2)]
        mix_w = cast_w[n_ffn:]
        sp_list.append(sp)
        ss_list.append(ss)
        vs_list.append(vs.reshape(bs, ls, GMLP_WIDTH))
    xp, = _ffn(xp, stages, fw, tm=_ffn_tile(stages), final=True)
    xs, = _ffn(xs, stages, fw, tm=SAMPLE_FFN_TILE, final=True)
    y_prompt = xp.reshape(bp, lp, D_MODEL)
    y_sample = xs.reshape(bs, ls, D_MODEL)
    return (y_prompt, y_sample, jnp.stack(sp_list), jnp.stack(ss_list), jnp.stack(vs_list))
```

```python
import functools

import jax
import jax.numpy as jnp
from jax import lax
from jax.experimental import pallas as pl
from jax.experimental.pallas import tpu as pltpu

D_MODEL = 1024
DEPTH = 4
PAST_LEN = 4096
RET_HEADS = 4
RET_KEY_DIM = 128
RET_VAL_DIM = 128
RET_WIDTH = RET_HEADS * RET_KEY_DIM
GMLP_CHUNK = 128
GMLP_GROUPS = 4
GMLP_WIDTH = 512
D_FF = 2816
ROPE_BASE = 10000.0
EPS = 1e-6
IN_COLS = 4 * RET_WIDTH + 2 * GMLP_WIDTH + 2 * D_MODEL

LANES = 128
FF_CHUNK = 256
VMEM_LIMIT_BYTES = 56 * 1024 * 1024

RET_BLOCK = 256
PROMPT_TILE = 512
PROMPT_PARTS = 2
SAMPLE_ROWS = 16
SAMPLE_FFN_TILE = 512

F32 = jnp.float32
BF16 = jnp.bfloat16


def _const_spec(shape, layer=None):
    nd = len(shape)
    if layer is None:
        return pl.BlockSpec(shape, lambda *_: (0,) * nd, pipeline_mode=pl.Buffered(1))
    return pl.BlockSpec((None,) + shape, lambda *_: (layer,) + (0,) * nd,
                        pipeline_mode=pl.Buffered(1))


BF16_SUBLANES = 16


def _cast_specs(casts, n_steps, linear_step):
    in_specs, args, out_specs, out_shapes, has_gain = [], [], [], [], []
    for src, layer, gain in casts:
        _, n_rows, n_cols = src.shape
        rb = BF16_SUBLANES
        while n_rows % rb or n_rows // rb > n_steps:
            rb += BF16_SUBLANES
        n_blk = n_rows // rb
        hold = n_steps // n_blk

        def blk(*g, hold=hold, n_blk=n_blk):
            return jnp.minimum(linear_step(*g) // hold, n_blk - 1)

        in_specs.append(pl.BlockSpec((None, rb, n_cols), lambda *g, blk=blk, layer=layer: (layer, blk(*g), 0)))
        args.append(src)
        if gain is not None:
            in_specs.append(pl.BlockSpec((None, rb, 1), lambda *g, blk=blk, layer=layer: (layer, blk(*g), 0)))
            args.append(gain)
        out_specs.append(pl.BlockSpec((rb, n_cols), lambda *g, blk=blk: (blk(*g), 0)))
        out_shapes.append(jax.ShapeDtypeStruct((n_rows, n_cols), BF16))
        has_gain.append(gain is not None)
    return in_specs, args, out_specs, out_shapes, tuple(has_gain)


def _run_casts(in_refs, out_refs, has_gain):
    it = iter(in_refs)
    for dst_ref, gained in zip(out_refs, has_gain):
        w = next(it)[...]
        if gained:
            w = w * next(it)[...]
        dst_ref[...] = w.astype(BF16)


def _rms_scale(x):
    return lax.rsqrt(jnp.mean(x * x, axis=-1, keepdims=True) + EPS)


def _rms(x, w):
    return x * _rms_scale(x) * w


def _sigmoid(x):
    return 0.5 * jnp.tanh(0.5 * x) + 0.5


def _silu(x):
    h = 0.5 * x
    return h * jnp.tanh(h) + h


def _ffn_kernel(*refs, n_stage, final, has_gain):
    x_ref = refs[0]
    stage_refs = refs[1:1 + 2 * n_stage]
    fw_ref = refs[1 + 2 * n_stage]
    n_cast_in = len(has_gain) + sum(has_gain)
    cast_in = refs[2 + 2 * n_stage:2 + 2 * n_stage + n_cast_in]
    o_ref = refs[2 + 2 * n_stage + n_cast_in]
    cast_out = refs[3 + 2 * n_stage + n_cast_in:3 + 2 * n_stage + n_cast_in + len(has_gain)]
    h_ref, act_ref = refs[3 + 2 * n_stage + n_cast_in + len(has_gain):]
    sub = h_ref.shape[0]
    for part in range(x_ref.shape[0] // sub):
        rs = slice(part * sub, (part + 1) * sub)
        src_ref = x_ref.at[rs]
        dst_ref = o_ref.at[rs]
        for s in range(n_stage):
            wgu_ref, wd_ref = stage_refs[2 * s:2 * s + 2]
            x = src_ref[...]
            h_ref[...] = x.astype(BF16)
            rinv = _rms_scale(x)
            for c in range(D_FF // FF_CHUNK):
                if part == 0 and s == 0 and c == 2:
                    _run_casts(cast_in, cast_out, has_gain)
                cols = slice(c * FF_CHUNK, (c + 1) * FF_CHUNK)
                a = rinv * jnp.dot(h_ref[...], wgu_ref[:, cols], preferred_element_type=F32)
                b = rinv * jnp.dot(h_ref[...], wgu_ref[:, D_FF + c * FF_CHUNK:D_FF + (c + 1) * FF_CHUNK],
                                   preferred_element_type=F32)
                act_ref[:, cols] = (_silu(a) * b).astype(BF16)
            y = jnp.dot(act_ref[...], wd_ref[...], preferred_element_type=F32)
            out = src_ref[...] + 0.5 * y
            if final and s == n_stage - 1:
                out = _rms(out, fw_ref[...])
            dst_ref[...] = out
            src_ref = dst_ref


def _ffn_tile(stages):
    return (1024, 1024) if len(stages) == 1 else (1024, 256)


def _ffn(x, stages, final_w, *, tm, final, casts=()):
    tm, sub = tm if isinstance(tm, tuple) else (tm, tm)
    n = x.shape[0]
    in_specs = [pl.BlockSpec((tm, D_MODEL), lambda i: (i, 0))]
    args = [x]
    for w_gu, w_down in stages:
        in_specs += [_const_spec((D_MODEL, 2 * D_FF)), _const_spec((D_FF, D_MODEL))]
        args += [w_gu, w_down]
    in_specs.append(_const_spec((1, D_MODEL)))
    args.append(final_w)
    c_in, c_args, c_out, c_shapes, has_gain = _cast_specs(casts, n // tm, lambda i: i)
    return pl.pallas_call(
        functools.partial(_ffn_kernel, n_stage=len(stages), final=final, has_gain=has_gain),
        out_shape=[jax.ShapeDtypeStruct((n, D_MODEL), F32)] + c_shapes,
        grid=(n // tm,),
        in_specs=in_specs + c_in,
        out_specs=[pl.BlockSpec((tm, D_MODEL), lambda i: (i, 0))] + c_out,
        scratch_shapes=[pltpu.VMEM((sub, D_MODEL), BF16), pltpu.VMEM((sub, D_FF), BF16)],
        compiler_params=pltpu.CompilerParams(
            dimension_semantics=("arbitrary",), vmem_limit_bytes=VMEM_LIMIT_BYTES),
        name="ffn%d%s" % (len(stages), "_final" if final else ""),
    )(*args, *c_args)


def _table_kernel(freq_ref, logg_ref, cos_ref, sin_ref, dq_ref, dk_ref, bc_ref, bs_ref, *, pos0, period, seg):
    n = cos_ref.shape[0]
    pid = pl.program_id(0)

    @pl.when(pid == 0)
    def _():
        jb = jnp.bitwise_and(lax.broadcasted_iota(jnp.int32, (n, LANES), 0), period - 1).astype(F32)
        ang = jb * freq_ref[...]
        bc_ref[...] = jnp.cos(ang)
        bs_ref[...] = jnp.sin(ang)
        m = dq_ref.shape[0]
        j = jnp.bitwise_and(lax.broadcasted_iota(jnp.int32, (m, RET_WIDTH), 0), seg - 1).astype(F32)
        e = (j + 1.0) * logg_ref[...]
        dq_ref[...] = jnp.exp(e)
        dk_ref[...] = jnp.exp(-e) * (RET_KEY_DIM ** -0.5)

    off = (pos0 + pid * n).astype(F32) * freq_ref[...]
    co = jnp.cos(off)
    so = jnp.sin(off)
    bc = bc_ref[...]
    bs = bs_ref[...]
    cos_ref[...] = bc * co - bs * so
    s = bs * co + bc * so
    lane = lax.broadcasted_iota(jnp.int32, (n, LANES), 1)
    sin_ref[...] = jnp.where(lane < LANES // 2, -s, s)


def _tables(freq2, logg_lanes, *, n_pos, pos0, period, tm, seg):
    assert period & (period - 1) == 0 and seg & (seg - 1) == 0
    tb = min(n_pos, 512)
    assert period >= n_pos or n_pos == tb
    return pl.pallas_call(
        functools.partial(_table_kernel, pos0=pos0, period=period, seg=seg),
        out_shape=(
            jax.ShapeDtypeStruct((n_pos, LANES), F32),
            jax.ShapeDtypeStruct((n_pos, LANES), F32),
            jax.ShapeDtypeStruct((tm, RET_WIDTH), F32),
            jax.ShapeDtypeStruct((tm, RET_WIDTH), F32),
        ),
        grid=(n_pos // tb,),
        in_specs=[pl.BlockSpec((1, LANES), lambda i: (0, 0)),
                  pl.BlockSpec((1, RET_WIDTH), lambda i: (0, 0))],
        out_specs=(
            pl.BlockSpec((tb, LANES), lambda i: (i, 0)),
            pl.BlockSpec((tb, LANES), lambda i: (i, 0)),
            pl.BlockSpec((tm, RET_WIDTH), lambda i: (0, 0)),
            pl.BlockSpec((tm, RET_WIDTH), lambda i: (0, 0)),
        ),
        scratch_shapes=[pltpu.VMEM((tb, LANES), F32), pltpu.VMEM((tb, LANES), F32)],
        compiler_params=pltpu.CompilerParams(dimension_semantics=("arbitrary",)),
        name="tables",
    )(freq2, logg_lanes)


def _mix_kernel(*refs, tm, parts, dense, seg, gseg, rows, carry, has_s0, emit_vn, has_gain):
    it = iter(refs)
    x_ref, cos_ref, sin_ref, dq_ref, dk_ref = (next(it) for _ in range(5))
    gnw_ref, lnw_ref, lnb_ref, ws_ref, bst_ref = (next(it) for _ in range(5))
    win_ref, wbr_ref, wbm_ref, wout_ref = (next(it) for _ in range(4))
    s0_ref = next(it) if has_s0 else None
    cast_in = [next(it) for _ in range(len(has_gain) + sum(has_gain))]
    o_ref, st_ref = next(it), next(it)
    vn_ref = next(it) if emit_vn else None
    cast_out = [next(it) for _ in has_gain]
    scratch = [next(it) for _ in range(10)]
    vn_scratch = None if emit_vn else next(it)

    if carry:
        @pl.when(pl.program_id(1) == 0)
        def _():
            st_ref[...] = s0_ref[...] if has_s0 else jnp.zeros(st_ref.shape, F32)
    else:
        st_ref[...] = s0_ref[...] if has_s0 else jnp.zeros(st_ref.shape, F32)

    for part in range(parts):
        rs = slice(part * tm, (part + 1) * tm)
        casts = functools.partial(_run_casts, cast_in, cast_out, has_gain) if part == 0 else None
        _mix_tile(x_ref.at[rs], cos_ref.at[rs], sin_ref.at[rs], dq_ref, dk_ref,
                  gnw_ref, lnw_ref, lnb_ref, ws_ref, bst_ref, win_ref, wbr_ref, wbm_ref, wout_ref,
                  o_ref.at[rs], st_ref, vn_ref.at[rs] if emit_vn else vn_scratch, scratch, casts,
                  tm=tm, dense=dense, seg=seg, gseg=gseg, rows=rows)


def _mix_tile(x_ref, cos_ref, sin_ref, dq_ref, dk_ref, gnw_ref, lnw_ref, lnb_ref, ws_ref, bst_ref,
              win_ref, wbr_ref, wbm_ref, wout_ref, o_ref, st_ref, vn_ref, scratch, casts,
              *, tm, dense, seg, gseg, rows):
    h_ref, qd_ref, kd_ref, v_ref, sg_ref, ret_ref, mrg_ref, u_ref, gat_ref, gm_ref = scratch

    x = x_ref[...]
    h_ref[...] = x.astype(BF16)
    rinv = _rms_scale(x)

    c_uv = 4 * RET_WIDTH
    c_ar = c_uv + 2 * GMLP_WIDTH
    c_am = c_ar + D_MODEL

    def proj(lo, hi):
        return rinv * jnp.dot(h_ref[...], win_ref[:, lo:hi], preferred_element_type=F32)

    qk = proj(0, 2 * RET_WIDTH)
    cos = cos_ref[...]
    sin = sin_ref[...]
    for hh in range(RET_HEADS):
        sl = slice(hh * LANES, (hh + 1) * LANES)
        q = qk[:, hh * LANES:(hh + 1) * LANES]
        k = qk[:, RET_WIDTH + hh * LANES:RET_WIDTH + (hh + 1) * LANES]
        q = q * cos + pltpu.roll(q, LANES // 2, 1) * sin
        k = k * cos + pltpu.roll(k, LANES // 2, 1) * sin
        qd_ref[:, sl] = (q * dq_ref[:, sl]).astype(BF16)
        kd_ref[:, sl] = (k * dk_ref[:, sl]).astype(BF16)

    vg = proj(2 * RET_WIDTH, 4 * RET_WIDTH)
    v_ref[...] = vg[:, :RET_WIDTH].astype(BF16)
    g = vg[:, RET_WIDTH:]
    sg_ref[...] = _silu(g)

    ri = lax.broadcasted_iota(jnp.int32, (dense, dense), 0)
    ci = lax.broadcasted_iota(jnp.int32, (dense, dense), 1)
    mask = ci <= ri
    if seg < dense:
        shift = seg.bit_length() - 1
        mask = mask & ((ri >> shift) == (ci >> shift))
    n_seg = dense // seg
    n_blk = tm // dense

    def retention_block(blk):
        r0 = blk * dense
        for hh in range(RET_HEADS):
            sl = slice(hh * LANES, (hh + 1) * LANES)
            qd = qd_ref[r0:r0 + dense, sl]
            kd = kd_ref[r0:r0 + dense, sl]
            vv = v_ref[r0:r0 + dense, sl]
            sc = lax.dot_general(qd, kd, (((1,), (1,)), ((), ())), preferred_element_type=F32)
            sc = jnp.where(mask, sc, 0.0).astype(BF16)
            intra = jnp.dot(sc, vv, preferred_element_type=F32)
            gam_seg = (1.0 - 2.0 ** (-5.0 - hh)) ** seg
            for sg_i in range(n_seg):
                a0 = sg_i * seg
                row = (blk * n_seg + sg_i) if rows > 1 else 0
                st = st_ref[row, hh]
                cross = jnp.dot(qd[a0:a0 + seg], st.astype(BF16), preferred_element_type=F32)
                ret_ref[r0 + a0:r0 + a0 + seg, sl] = intra[a0:a0 + seg] + cross
                kv = lax.dot_general(kd[a0:a0 + seg], vv[a0:a0 + seg], (((0,), (0,)), ((), ())),
                                     preferred_element_type=F32)
                st_ref[row, hh] = gam_seg * (st + kv)

    for blk in range((n_blk + 1) // 2):
        retention_block(blk)

    uv = proj(c_uv, c_ar)
    u_ref[...] = uv[:, :GMLP_WIDTH]
    vm = uv[:, GMLP_WIDTH:]
    vc = vm - jnp.mean(vm, axis=-1, keepdims=True)
    vn = vc * lax.rsqrt(jnp.mean(vc * vc, axis=-1, keepdims=True) + EPS)
    vn_ref[...] = vn * lnw_ref[...] + lnb_ref[...]

    mrg_ref[...] = _sigmoid(proj(c_ar, c_am))

    for blk in range((n_blk + 1) // 2, n_blk):
        retention_block(blk)

    if casts is not None:
        casts()

    for hh in range(RET_HEADS):
        sl = slice(hh * LANES, (hh + 1) * LANES)
        o = ret_ref[:, sl]
        oc = o - jnp.mean(o, axis=-1, keepdims=True)
        on = oc * lax.rsqrt(jnp.mean(oc * oc, axis=-1, keepdims=True) + EPS)
        gat_ref[:, sl] = (on * gnw_ref[:, sl] * sg_ref[:, sl]).astype(BF16)

    n_g = tm // gseg
    wi = lax.broadcasted_iota(jnp.int32, (gseg, gseg), 0)
    wj = lax.broadcasted_iota(jnp.int32, (gseg, gseg), 1)
    for gg in range(GMLP_GROUPS):
        sl = slice(gg * LANES, (gg + 1) * LANES)
        w = jnp.where(wj <= wi, ws_ref[gg, 0:gseg, 0:gseg], 0.0).astype(BF16)
        rhs = jnp.concatenate([vn_ref[c * gseg:(c + 1) * gseg, sl] for c in range(n_g)], axis=1)
        sp = jnp.dot(w, rhs.astype(BF16), preferred_element_type=F32)
        bias = bst_ref[0:gseg, gg:gg + 1]
        for c in range(n_g):
            s_c = sp[:, c * LANES:(c + 1) * LANES] + bias
            gm_ref[c * gseg:(c + 1) * gseg, sl] = (u_ref[c * gseg:(c + 1) * gseg, sl] * s_c).astype(BF16)

    g_mlp = _sigmoid(proj(c_am, IN_COLS))
    y_ret = jnp.dot(gat_ref[...], wbr_ref[...], preferred_element_type=F32)
    y_mlp = jnp.dot(gm_ref[...], wbm_ref[...], preferred_element_type=F32)
    merged = (mrg_ref[...] * y_ret + g_mlp * y_mlp).astype(BF16)
    o_ref[...] = x_ref[...] + jnp.dot(merged, wout_ref[...], preferred_element_type=F32)


def _mix(x, tabs, gn_w, ln_w, ln_b, ws, bst, weights, s0, layer, *,
         tm, parts, dense, seg, gseg, rows, n_outer, n_inner, carry, emit_vn, casts=()):
    w_in, w_br, w_bm, w_out = weights
    cos, sin, dq, dk = tabs
    has_s0 = s0 is not None
    n_tok = x.shape[0]
    n_state = n_outer * rows
    tb = parts * tm
    x_spec = pl.BlockSpec((tb, D_MODEL), lambda b, t: (b * n_inner + t, 0))
    if cos.shape[0] == tb:
        tab_spec = pl.BlockSpec((tb, LANES), lambda b, t: (0, 0))
    else:
        tab_spec = pl.BlockSpec((tb, LANES), lambda b, t: (t, 0))
    st_spec = pl.BlockSpec((rows, RET_HEADS, RET_KEY_DIM, RET_VAL_DIM), lambda b, t: (b, 0, 0, 0))
    in_specs = [
        x_spec, tab_spec, tab_spec,
        _const_spec((tm, RET_WIDTH)), _const_spec((tm, RET_WIDTH)),
        _const_spec((1, RET_WIDTH), layer),
        _const_spec((1, GMLP_WIDTH), layer), _const_spec((1, GMLP_WIDTH), layer),
        _const_spec((GMLP_GROUPS, GMLP_CHUNK, GMLP_CHUNK), layer),
        _const_spec((GMLP_CHUNK, GMLP_GROUPS), layer),
        _const_spec((D_MODEL, IN_COLS)), _const_spec((RET_WIDTH, D_MODEL)),
        _const_spec((GMLP_WIDTH, D_MODEL)), _const_spec((D_MODEL, D_MODEL)),
    ]
    args = [x, cos, sin, dq, dk, gn_w, ln_w, ln_b, ws, bst, w_in, w_br, w_bm, w_out]
    if has_s0:
        in_specs.append(pl.BlockSpec((None, rows, RET_HEADS, RET_KEY_DIM, RET_VAL_DIM),
                                     lambda b, t: (layer, b, 0, 0, 0)))
        args.append(s0)
    out_shape = [jax.ShapeDtypeStruct((n_tok, D_MODEL), F32),
                 jax.ShapeDtypeStruct((n_state, RET_HEADS, RET_KEY_DIM, RET_VAL_DIM), F32)]
    out_specs = [x_spec, st_spec]
    scratch = [
        pltpu.VMEM((tm, D_MODEL), BF16),
        pltpu.VMEM((tm, RET_WIDTH), BF16),
        pltpu.VMEM((tm, RET_WIDTH), BF16),
        pltpu.VMEM((tm, RET_WIDTH), BF16),
        pltpu.VMEM((tm, RET_WIDTH), F32),
        pltpu.VMEM((tm, RET_WIDTH), F32),
        pltpu.VMEM((tm, D_MODEL), F32),
        pltpu.VMEM((tm, GMLP_WIDTH), F32),
        pltpu.VMEM((tm, RET_WIDTH), BF16),
        pltpu.VMEM((tm, GMLP_WIDTH), BF16),
    ]
    if emit_vn:
        out_shape.append(jax.ShapeDtypeStruct((n_tok, GMLP_WIDTH), F32))
        out_specs.append(pl.BlockSpec((tb, GMLP_WIDTH), lambda b, t: (b * n_inner + t, 0)))
    else:
        scratch.append(pltpu.VMEM((tm, GMLP_WIDTH), F32))
    c_in, c_args, c_out, c_shapes, has_gain = _cast_specs(
        casts, n_outer * n_inner, lambda b, t: b * n_inner + t)
    return pl.pallas_call(
        functools.partial(_mix_kernel, tm=tm, parts=parts, dense=dense, seg=seg, gseg=gseg, rows=rows,
                          carry=carry, has_s0=has_s0, emit_vn=emit_vn, has_gain=has_gain),
        out_shape=out_shape + c_shapes,
        grid=(n_outer, n_inner),
        in_specs=in_specs + c_in,
        out_specs=out_specs + c_out,
        scratch_shapes=scratch,
        compiler_params=pltpu.CompilerParams(
            dimension_semantics=("arbitrary", "arbitrary"), vmem_limit_bytes=VMEM_LIMIT_BYTES),
        name="mix_rows" if rows > 1 else "mix_seq",
    )(*args, *c_args)


def kernel(x_prompt, x_sample, state_ret, ffn1_norm, ffn1_w_gu, ffn1_w_down, mix_norm, w_in, ret_gn_w, gmlp_ln_w, gmlp_ln_b, gmlp_ws, gmlp_bs, w_br_ret, w_br_mlp, w_out, ffn2_norm, ffn2_w_gu, ffn2_w_down, final_norm):
    bp, lp, _ = x_prompt.shape
    bs, ls, _ = x_sample.shape

    g1 = ffn1_norm[:, :, None]
    g2 = ffn2_norm[:, :, None]
    gm = mix_norm[:, :, None]
    stages = [((g1[0] * ffn1_w_gu[0]).astype(BF16), ffn1_w_down[0].astype(BF16))]
    gnw = ret_gn_w.reshape(DEPTH, 1, RET_WIDTH)
    lnw = gmlp_ln_w.reshape(DEPTH, 1, GMLP_WIDTH)
    lnb = gmlp_ln_b.reshape(DEPTH, 1, GMLP_WIDTH)
    bst = jnp.swapaxes(gmlp_bs, 1, 2)
    fw = final_norm.reshape(1, D_MODEL)

    half = RET_KEY_DIM // 2
    freqs = ROPE_BASE ** (-jnp.arange(half, dtype=F32) / half)
    freq2 = jnp.concatenate([freqs, freqs]).reshape(1, LANES)
    log_g = jnp.log1p(-jnp.exp2(-5.0 - jnp.arange(RET_HEADS, dtype=F32)))
    logg_lanes = jnp.repeat(log_g, LANES).reshape(1, RET_WIDTH)

    tm_p, parts_p, dense_p = PROMPT_TILE, PROMPT_PARTS, RET_BLOCK
    rows_s = SAMPLE_ROWS
    tm_s = rows_s * ls
    assert lp % (parts_p * tm_p) == 0 and tm_p % dense_p == 0 and bs % rows_s == 0
    tabs_p = _tables(freq2, logg_lanes, n_pos=lp, pos0=0, period=lp, tm=tm_p, seg=dense_p)
    tabs_s = _tables(freq2, logg_lanes, n_pos=tm_s, pos0=PAST_LEN, period=ls, tm=tm_s, seg=ls)

    def mixer_casts(l):
        return [(w_in, l, gm), (w_br_ret, l, None), (w_br_mlp, l, None), (w_out, l, None)]

    xp = x_prompt.reshape(bp * lp, D_MODEL)
    xs = x_sample.reshape(bs * ls, D_MODEL)
    sp_list, ss_list, vs_list = [], [], []
    for l in range(DEPTH):
        xp, *cast_w = _ffn(xp, stages, fw, tm=_ffn_tile(stages), final=False,
                           casts=mixer_casts(0) if l == 0 else ())
        if l == 0:
            mix_w = cast_w
        xs, = _ffn(xs, stages, fw, tm=SAMPLE_FFN_TILE, final=False)
        next_casts = [(ffn2_w_gu, l, g2), (ffn2_w_down, l, None)]
        if l + 1 < DEPTH:
            next_casts += [(ffn1_w_gu, l + 1, g1), (ffn1_w_down, l + 1, None)] + mixer_casts(l + 1)
        xp, sp, *cast_w = _mix(xp, tabs_p, gnw, lnw, lnb, gmlp_ws, bst, mix_w,
                               None, l, tm=tm_p, parts=parts_p, dense=dense_p, seg=dense_p,
                               gseg=min(GMLP_CHUNK, lp), rows=1, n_outer=bp, n_inner=lp // (parts_p * tm_p),
                               carry=True, emit_vn=False, casts=next_casts)
        xs, ss, vs = _mix(xs, tabs_s, gnw, lnw, lnb, gmlp_ws, bst, mix_w,
                          state_ret, l, tm=tm_s, parts=1, dense=tm_s, seg=ls, gseg=min(GMLP_CHUNK, ls), rows=rows_s,
                          n_outer=bs // rows_s, n_inner=1, carry=False, emit_vn=True)
        n_ffn = 4 if l + 1 < DEPTH else 2
        stages = [tuple(cast_w[i:i + 2]) for i in range(0, n_ffn, 2)]
        mix_w = cast_w[n_ffn:]
        sp_list.append(sp)
        ss_list.append(ss)
        vs_list.append(vs.reshape(bs, ls, GMLP_WIDTH))
    xp, = _ffn(xp, stages, fw, tm=_ffn_tile(stages), final=True)
    xs, = _ffn(xs, stages, fw, tm=SAMPLE_FFN_TILE, final=True)
    y_prompt = xp.reshape(bp, lp, D_MODEL)
    y_sample = xs.reshape(bs, ls, D_MODEL)
    return (y_prompt, y_sample, jnp.stack(sp_list), jnp.stack(ss_list), jnp.stack(vs_list))
```
